```python
import math
import jax, jax.numpy as jnp
from jax import lax
import numpy as np

D_MODEL = 2048
BATCH = 16
SEQ = 256
DEPTH = 2
DEC_BATCH = 4
DEC_SEQ = 1024
PAST_LEN = 512

GRID_W = 64
BRANCH_WIDTH = D_MODEL // 2
DIFF_HEAD_DIM = 128
DIFF_QK_DIM = DIFF_HEAD_DIM // 2
DIFF_HEADS = BRANCH_WIDTH // DIFF_HEAD_DIM
GQA_HEAD_DIM = 128
GQA_HEADS = BRANCH_WIDTH // GQA_HEAD_DIM
GQA_KV_HEADS = GQA_HEADS // 4
GLA_DK = 128
GLA_DV = 256
GLA_HEADS = BRANCH_WIDTH // GLA_DV
GLA_GATE_RANK = 16
GLA_TAU = 16.0
GLA_CHUNK = 64
N_BRANCHES = 3
N_EXPERTS = 16
EC_CAPACITY_FACTOR = 2
D_EXPERT = D_MODEL
N_MOD = 6
Q_BLOCK = 128
ROPE_THETA = 10000.0
EPS = 1e-6

SPLIT_SIZES = (
    DIFF_HEADS * DIFF_HEAD_DIM,
    DIFF_HEADS * DIFF_HEAD_DIM,
    DIFF_HEADS * DIFF_HEAD_DIM,
    GQA_HEADS * GQA_HEAD_DIM,
    GQA_KV_HEADS * GQA_HEAD_DIM,
    GQA_KV_HEADS * GQA_HEAD_DIM,
    GLA_HEADS * GLA_DK,
    GLA_HEADS * GLA_DK,
    GLA_HEADS * GLA_DV,
    GLA_HEADS * GLA_DV,
    2 * GLA_GATE_RANK,
)
SPLIT_POINTS = tuple(sum(SPLIT_SIZES[: i + 1]) for i in range(len(SPLIT_SIZES) - 1))
W_IN = sum(SPLIT_SIZES)

kernel_name = "hybrid_diffattn_gqa_gla_ec_denoise_step"


def rms_norm(x, g):
    xf = x.astype(jnp.float32)
    y = xf * lax.rsqrt(jnp.mean(xf * xf, axis=-1, keepdims=True) + EPS)
    return (y * g.astype(jnp.float32)).astype(x.dtype)


def _heads(x, n):
    B, T, _ = x.shape
    return x.reshape(B, T, n, -1).transpose(0, 2, 1, 3)


def _merge_heads(x):
    B, H, T, d = x.shape
    return x.transpose(0, 2, 1, 3).reshape(B, T, H * d)


def _rope_angles(pos, dim):
    inv = ROPE_THETA ** (-jnp.arange(0, dim, 2, dtype=jnp.float32) / dim)
    return pos.astype(jnp.float32)[:, None] * inv[None, :]


def axial_rope_tables(T, dim):
    rows = T // GRID_W
    row = jnp.repeat(jnp.arange(rows), GRID_W)
    col = jnp.tile(jnp.arange(GRID_W), rows)
    ang = jnp.concatenate([_rope_angles(row, dim // 2), _rope_angles(col, dim // 2)], axis=-1)
    return jnp.cos(ang), jnp.sin(ang)


def apply_rope(x, cos, sin):
    x1, x2 = x[..., 0::2], x[..., 1::2]
    c, s = cos.astype(x.dtype), sin.astype(x.dtype)
    return jnp.stack([x1 * c - x2 * s, x1 * s + x2 * c], axis=-1).reshape(x.shape)


def ada_modulation(cond, w_mod, b_mod):
    m = jax.nn.silu(cond) @ w_mod + b_mod
    m = m.reshape(cond.shape[0], N_MOD, 1, D_MODEL)
    return tuple(m[:, i] for i in range(N_MOD))


def mixer_projections(h, p):
    B, T, _ = h.shape
    dq, dk, dv, gq, gk, gv, lq, lk, lv, lr, la = jnp.split(h @ p["w_in"], SPLIT_POINTS, axis=-1)

    def diff_norm(x, g):
        xh = _heads(x, DIFF_HEADS).reshape(B, DIFF_HEADS, T, 2, DIFF_QK_DIM)
        return rms_norm(xh, g).reshape(B, DIFF_HEADS, T, DIFF_HEAD_DIM)

    def log_decay(z, d):
        logits = (z @ p["gla_wa2"][d] + p["gla_ba"][d]).astype(jnp.float32)
        return _heads(jax.nn.log_sigmoid(logits) / GLA_TAU, GLA_HEADS)

    la_f, la_b = jnp.split(la, 2, axis=-1)
    return dict(
        dq=diff_norm(dq, p["diff_q_norm"]), dk=diff_norm(dk, p["diff_k_norm"]), dv=_heads(dv, DIFF_HEADS),
        gq=rms_norm(_heads(gq, GQA_HEADS), p["gqa_q_norm"]),
        gk=rms_norm(_heads(gk, GQA_KV_HEADS), p["gqa_k_norm"]), gv=_heads(gv, GQA_KV_HEADS),
        lq=_heads(lq, GLA_HEADS) * (GLA_DK ** -0.5), lk=_heads(lk, GLA_HEADS), lv=_heads(lv, GLA_HEADS),
        lr=lr, la_f=log_decay(la_f, 0), la_b=log_decay(la_b, 1),
    )


def sweep_query_blocks(fn, q):
    B, H, T, d = q.shape
    nb = T // Q_BLOCK
    qb = q.reshape(B, H, nb, Q_BLOCK, d).transpose(2, 0, 1, 3, 4)
    ob = lax.map(fn, qb)
    return ob.transpose(1, 2, 0, 3, 4).reshape(B, H, T, ob.shape[-1])


def _diff_lambda_value(lam_vecs, lam_init):
    lv = lam_vecs.astype(jnp.float32)
    return jnp.exp(jnp.sum(lv[0] * lv[1])) - jnp.exp(jnp.sum(lv[2] * lv[3])) + lam_init


def diff_attention_block(qb, k, v, lam):
    q1, q2 = jnp.split(qb, 2, axis=-1)
    k1, k2 = jnp.split(k, 2, axis=-1)
    scale = DIFF_QK_DIM ** -0.5
    s1 = jnp.einsum("bhqd,bhkd->bhqk", q1, k1, preferred_element_type=jnp.float32) * scale
    s2 = jnp.einsum("bhqd,bhkd->bhqk", q2, k2, preferred_element_type=jnp.float32) * scale
    a = jax.nn.softmax(s1, axis=-1) - lam * jax.nn.softmax(s2, axis=-1)
    return jnp.einsum("bhqk,bhkd->bhqd", a.astype(v.dtype), v)


def gqa_block(qb, k, v):
    B, H, Q, d = qb.shape
    G = k.shape[1]
    qg = qb.reshape(B, G, H // G, Q, d)
    s = jnp.einsum("bgrqd,bgkd->bgrqk", qg, k, preferred_element_type=jnp.float32) * (d ** -0.5)
    pr = jax.nn.softmax(s, axis=-1).astype(v.dtype)
    return jnp.einsum("bgrqk,bgkd->bgrqd", pr, v).reshape(B, H, Q, v.shape[-1])


def gla_scan(q, k, v, log_a, s0):
    B, H, T, dk = q.shape
    dv = v.shape[-1]
    n = T // GLA_CHUNK

    def chunks(x):
        return x.reshape(B, H, n, GLA_CHUNK, x.shape[-1]).transpose(2, 0, 1, 3, 4)

    causal = jnp.tril(jnp.ones((GLA_CHUNK, GLA_CHUNK), dtype=bool))[:, :, None]

    def step(S, inp):
        qc, kc, vc, ac = inp
        qf, kf, vf = qc.astype(jnp.float32), kc.astype(jnp.float32), vc.astype(jnp.float32)
        b = jnp.cumsum(ac.astype(jnp.float32), axis=2)
        inter = jnp.einsum("bhtk,bhkv->bhtv", qf * jnp.exp(b), S)
        rel = jnp.exp(jnp.where(causal, b[:, :, :, None, :] - b[:, :, None, :, :], -jnp.inf))
        scores = jnp.einsum("bhtk,bhsk,bhtsk->bhts", qf, kf, rel)
        intra = jnp.einsum("bhts,bhsv->bhtv", scores, vf)
        b_last = b[:, :, -1:, :]
        S_new = jnp.exp(b_last[:, :, 0, :])[..., None] * S + jnp.einsum(
            "bhsk,bhsv->bhkv", kf * jnp.exp(b_last - b), vf)
        return S_new, (inter + intra).astype(v.dtype)

    S, o = lax.scan(step, s0.astype(jnp.float32), (chunks(q), chunks(k), chunks(v), chunks(log_a)))
    return o.transpose(1, 2, 0, 3, 4).reshape(B, H, T, dv), S


def _flip(x):
    return x[:, :, ::-1]


def bidirectional_gla(t, s0_f, s0_b, g_norm):
    o_f, s_f = gla_scan(t["lq"], t["lk"], t["lv"], t["la_f"], s0_f)
    o_b, s_b = gla_scan(_flip(t["lq"]), _flip(t["lk"]), _flip(t["lv"]), _flip(t["la_b"]), s0_b)
    o = o_f + _flip(o_b)
    out = _merge_heads(rms_norm(o, g_norm)) * jax.nn.silu(t["lr"])
    return out, s_f, s_b


def merge_branches(h, o_diff, o_gqa, o_gla, p):
    B, T, _ = h.shape
    outs = jnp.stack([o_diff, o_gqa, o_gla], axis=2)
    proj = jnp.einsum("btnw,nwd->btnd", outs, p["w_branch"])
    gates = jax.nn.sigmoid(h @ p["w_gate"] + p["b_gate"]).reshape(B, T, N_BRANCHES, D_MODEL)
    return jnp.sum(gates * proj, axis=2) @ p["w_out"]


def diff_output(o, p, lam_init):
    return _merge_heads(rms_norm(o, p["diff_subln"]) * (1.0 - lam_init))


def mixer_context(h, p, lam_init):
    B = h.shape[0]
    t = mixer_projections(h, p)
    lam = _diff_lambda_value(p["diff_lambda"], lam_init)
    o_diff = sweep_query_blocks(lambda qb: diff_attention_block(qb, t["dk"], t["dv"], lam), t["dq"])
    o_gqa = sweep_query_blocks(lambda qb: gqa_block(qb, t["gk"], t["gv"]), t["gq"])
    zeros = jnp.zeros((B, GLA_HEADS, GLA_DK, GLA_DV), jnp.float32)
    o_gla, s_f, s_b = bidirectional_gla(t, zeros, zeros, p["gla_norm"])
    y = merge_branches(h, diff_output(o_diff, p, lam_init), _merge_heads(o_gqa), o_gla, p)
    states = (t["dk"], t["dv"], t["gk"], t["gv"], s_f.astype(h.dtype), s_b.astype(h.dtype))
    return y, states


def mixer_latent(h, p, lam_init, ctx, rope):
    B, T, _ = h.shape
    ck_d, cv_d, ck_g, cv_g, s0_f, s0_b = ctx
    cos_d, sin_d, cos_g, sin_g = rope
    t = mixer_projections(h, p)

    def rope_diff(x):
        xv = x.reshape(B, DIFF_HEADS, T, 2, DIFF_QK_DIM)
        return apply_rope(xv, cos_d[:, None, :], sin_d[:, None, :]).reshape(x.shape)

    lam = _diff_lambda_value(p["diff_lambda"], lam_init)
    k_d = jnp.concatenate([ck_d.astype(h.dtype), rope_diff(t["dk"])], axis=2)
    v_d = jnp.concatenate([cv_d.astype(h.dtype), t["dv"]], axis=2)
    o_diff = sweep_query_blocks(lambda qb: diff_attention_block(qb, k_d, v_d, lam), rope_diff(t["dq"]))
    k_g = jnp.concatenate([ck_g.astype(h.dtype), apply_rope(t["gk"], cos_g, sin_g)], axis=2)
    v_g = jnp.concatenate([cv_g.astype(h.dtype), t["gv"]], axis=2)
    o_gqa = sweep_query_blocks(lambda qb: gqa_block(qb, k_g, v_g), apply_rope(t["gq"], cos_g, sin_g))
    o_gla, _, _ = bidirectional_gla(t, s0_f, s0_b, p["gla_norm"])
    return merge_branches(h, diff_output(o_diff, p, lam_init), _merge_heads(o_gqa), o_gla, p)


def expert_choice_ffn(h, p):
    B, T, D = h.shape
    cap = EC_CAPACITY_FACTOR * T // N_EXPERTS
    aff = jax.nn.softmax((h @ p["w_router"]).astype(jnp.float32), axis=-1)
    top_w, top_idx = lax.top_k(aff.transpose(0, 2, 1), cap)
    xg = jax.vmap(lambda hb, ib: hb[ib])(h, top_idx)
    hid = jax.nn.silu(jnp.einsum("becd,edf->becf", xg, p["w_e_gate"])) * jnp.einsum(
        "becd,edf->becf", xg, p["w_e_up"])
    ye = jnp.einsum("becf,efd->becd", hid, p["w_e_down"]) * top_w[..., None].astype(h.dtype)
    return jax.vmap(lambda hb, yb, ib: jnp.zeros_like(hb).at[ib.reshape(-1)].add(yb.reshape(-1, D)))(
        h, ye, top_idx)


def trunk_layer(x, mod, p, mixer_fn):
    shift1, scale1, gate1, shift2, scale2, gate2 = mod
    h = rms_norm(x, p["norm_mix"]) * (1 + scale1) + shift1
    y, aux = mixer_fn(h)
    x = x + gate1 * y
    h = rms_norm(x, p["norm_ffn"]) * (1 + scale2) + shift2
    x = x + gate2 * expert_choice_ffn(h, p)
    return x, aux


def setup_inputs(seed: int = 0) -> dict:
    key = jax.random.key(seed)
    ks = jax.random.split(key, 32)
    L, D = DEPTH, D_MODEL

    def nrm(k, shape, s):
        return s * jax.random.normal(k, shape, jnp.float32)

    def gain(k, shape):
        return 1.0 + nrm(k, shape, 0.02)

    return {
        "x_prompt": nrm(ks[0], (BATCH, SEQ, D), 1.0),
        "x_sample": nrm(ks[1], (DEC_BATCH, DEC_SEQ, D), 1.0),
        "cache_diff_k": nrm(ks[2], (DEC_BATCH, L, DIFF_HEADS, PAST_LEN, DIFF_HEAD_DIM), 1.0),
        "cache_diff_v": nrm(ks[3], (DEC_BATCH, L, DIFF_HEADS, PAST_LEN, DIFF_HEAD_DIM), 1.0),
        "cache_gqa_k": nrm(ks[4], (DEC_BATCH, L, GQA_KV_HEADS, PAST_LEN, GQA_HEAD_DIM), 1.0),
        "cache_gqa_v": nrm(ks[5], (DEC_BATCH, L, GQA_KV_HEADS, PAST_LEN, GQA_HEAD_DIM), 1.0),
        "state_gla_fwd": nrm(ks[6], (DEC_BATCH, L, GLA_HEADS, GLA_DK, GLA_DV), 1.0),
        "state_gla_bwd": nrm(ks[7], (DEC_BATCH, L, GLA_HEADS, GLA_DK, GLA_DV), 1.0),
        "c": nrm(ks[8], (DEC_BATCH, D), 1.0),
        "c_ctx": nrm(ks[9], (D,), 1.0),
        "w_mod": nrm(ks[10], (L, D, N_MOD * D), D ** -0.5),
        "b_mod": nrm(ks[11], (L, N_MOD * D), 0.02),
        "norm_mix": gain(ks[12], (L, D)),
        "w_in": nrm(ks[13], (L, D, W_IN), D ** -0.5),
        "diff_q_norm": gain(ks[14], (L, DIFF_QK_DIM)),
        "diff_k_norm": gain(ks[15], (L, DIFF_QK_DIM)),
        "diff_lambda": nrm(ks[16], (L, 4, DIFF_QK_DIM), 0.1),
        "diff_subln": gain(ks[17], (L, DIFF_HEAD_DIM)),
        "gqa_q_norm": gain(ks[18], (L, GQA_HEAD_DIM)),
        "gqa_k_norm": gain(ks[19], (L, GQA_HEAD_DIM)),
        "gla_wa2": nrm(ks[20], (L, 2, GLA_GATE_RANK, GLA_HEADS * GLA_DK), GLA_GATE_RANK ** -0.5),
        "gla_ba": nrm(ks[21], (L, 2, GLA_HEADS * GLA_DK), 0.02),
        "gla_norm": gain(ks[22], (L, GLA_DV)),
        "w_branch": nrm(ks[23], (L, N_BRANCHES, BRANCH_WIDTH, D), BRANCH_WIDTH ** -0.5),
        "w_gate": nrm(ks[24], (L, D, N_BRANCHES * D), D ** -0.5),
        "b_gate": nrm(ks[25], (L, N_BRANCHES * D), 0.02),
        "w_out": nrm(ks[26], (L, D, D), D ** -0.5),
        "norm_ffn": gain(ks[27], (L, D)),
        "w_router": nrm(ks[28], (L, D, N_EXPERTS), D ** -0.5),
        "w_e_gate": nrm(ks[29], (L, N_EXPERTS, D, D_EXPERT), D ** -0.5),
        "w_e_up": nrm(ks[30], (L, N_EXPERTS, D, D_EXPERT), D ** -0.5),
        "w_e_down": nrm(ks[31], (L, N_EXPERTS, D_EXPERT, D), D_EXPERT ** -0.5),
    }


def reference(x_prompt, x_sample, cache_diff_k, cache_diff_v, cache_gqa_k, cache_gqa_v,
              state_gla_fwd, state_gla_bwd, c, c_ctx, w_mod, b_mod, norm_mix, w_in,
              diff_q_norm, diff_k_norm, diff_lambda, diff_subln, gqa_q_norm, gqa_k_norm,
              gla_wa2, gla_ba, gla_norm, w_branch, w_gate, b_gate, w_out, norm_ffn,
              w_router, w_e_gate, w_e_up, w_e_down):
    T_lat = x_sample.shape[1]
    cos_d, sin_d = axial_rope_tables(T_lat, DIFF_QK_DIM)
    cos_g, sin_g = axial_rope_tables(T_lat, GQA_HEAD_DIM)
    rope = (cos_d, sin_d, cos_g, sin_g)

    xp, xs = x_prompt, x_sample
    st_dk, st_dv, st_gk, st_gv, st_sf, st_sb = [], [], [], [], [], []
    for l in range(DEPTH):
        p = dict(w_in=w_in[l], diff_q_norm=diff_q_norm[l], diff_k_norm=diff_k_norm[l],
                 diff_lambda=diff_lambda[l], diff_subln=diff_subln[l], gqa_q_norm=gqa_q_norm[l],
                 gqa_k_norm=gqa_k_norm[l], gla_wa2=gla_wa2[l], gla_ba=gla_ba[l], gla_norm=gla_norm[l],
                 w_branch=w_branch[l], w_gate=w_gate[l], b_gate=b_gate[l], w_out=w_out[l],
                 norm_mix=norm_mix[l], norm_ffn=norm_ffn[l], w_router=w_router[l],
                 w_e_gate=w_e_gate[l], w_e_up=w_e_up[l], w_e_down=w_e_down[l])
        lam_init = 0.8 - 0.6 * math.exp(-0.3 * l)

        mod_ctx = ada_modulation(c_ctx[None, :], w_mod[l], b_mod[l])
        xp, states = trunk_layer(xp, mod_ctx, p, lambda h: mixer_context(h, p, lam_init))
        for lst, s in zip((st_dk, st_dv, st_gk, st_gv, st_sf, st_sb), states):
            lst.append(s)

        mod_lat = ada_modulation(c, w_mod[l], b_mod[l])
        ctx = (cache_diff_k[:, l], cache_diff_v[:, l], cache_gqa_k[:, l], cache_gqa_v[:, l],
               state_gla_fwd[:, l], state_gla_bwd[:, l])
        xs, _ = trunk_layer(xs, mod_lat, p, lambda h: (mixer_latent(h, p, lam_init, ctx, rope), None))

    new_diff_k = jnp.stack(st_dk, axis=1)
    new_diff_v = jnp.stack(st_dv, axis=1)
    new_gqa_k = jnp.stack(st_gk, axis=1)
    new_gqa_v = jnp.stack(st_gv, axis=1)
    new_gla_fwd = jnp.stack(st_sf, axis=1)
    new_gla_bwd = jnp.stack(st_sb, axis=1)
    return (xp, xs, new_diff_k, new_diff_v, new_gqa_k, new_gqa_v, new_gla_fwd, new_gla_bwd)
```

```python
import functools
import math

import jax
import jax.numpy as jnp
from jax import lax
from jax.experimental import pallas as pl
from jax.experimental.pallas import tpu as pltpu

F32 = jnp.float32
BF16 = jnp.bfloat16

D_MODEL = 2048
BATCH = 16
SEQ = 256
DEPTH = 2
DEC_BATCH = 4
DEC_SEQ = 1024
PAST_LEN = 512
GRID_W = 64
BRANCH_WIDTH = 1024
HEAD = 128
DIFF_HEADS = 8
GQA_HEADS = 8
GQA_KV_HEADS = 2
GQA_GROUP = GQA_HEADS // GQA_KV_HEADS
GLA_HEADS = 4
GLA_DK = 128
GLA_DV = 256
GLA_GATE_RANK = 16
GLA_TAU = 16.0
GLA_BLOCK = 256
N_EXPERTS = 16
N_MOD = 6
ROPE_THETA = 10000.0
EPS = 1e-6

N_CTX = BATCH * SEQ
N_LAT = DEC_BATCH * DEC_SEQ
N_TOK = N_CTX + N_LAT
W_IN = 7712
LA_COL = 7680
COND_ROWS = 8

VMEM_LIMIT = 56 * 1024 * 1024


def _cparams(n_axes):
    return pltpu.CompilerParams(dimension_semantics=("arbitrary",) * n_axes,
                                vmem_limit_bytes=VMEM_LIMIT)


def _dot(a, b):
    return jnp.dot(a, b, preferred_element_type=F32)


def _dot_nt(a, b):
    return lax.dot_general(a, b, (((1,), (1,)), ((), ())), preferred_element_type=F32)


def _rms(x):
    return x * lax.rsqrt(jnp.mean(x * x, axis=-1, keepdims=True) + EPS)


def _cond_row(start):
    return jnp.where(start < N_CTX, 0, 1 + (start - N_CTX) // DEC_SEQ)


def _mod_spec(layer, chunk, tm):
    return pl.BlockSpec((None, None, 1, D_MODEL),
                        lambda i, *_: (layer, _cond_row(i * tm), 0, chunk))


def _mod_body(c_ref, w_ref, b_ref, o_ref):
    c = c_ref[...]
    a = (c * jax.nn.sigmoid(c)).astype(BF16)
    o_ref[...] = _dot(a, w_ref[...].astype(BF16)) + b_ref[...]


def _modulation(cond, w_mod, b_mod):
    n = N_MOD * D_MODEL
    tn = 1024
    return pl.pallas_call(
        _mod_body,
        grid=(DEPTH, n // tn),
        in_specs=[pl.BlockSpec((COND_ROWS, D_MODEL), lambda l, j: (0, 0)),
                  pl.BlockSpec((None, D_MODEL, tn), lambda l, j: (l, 0, j)),
                  pl.BlockSpec((None, 1, tn), lambda l, j: (l, 0, j))],
        out_specs=pl.BlockSpec((None, COND_ROWS, tn), lambda l, j: (l, 0, j)),
        out_shape=jax.ShapeDtypeStruct((DEPTH, COND_ROWS, n), F32),
        compiler_params=_cparams(2),
        name="ada_mod",
    )(cond, w_mod, b_mod.reshape(DEPTH, 1, n))


def _norm_mod_body(x_ref, g_ref, sc_ref, sh_ref, o_ref):
    y = _rms(x_ref[...]) * g_ref[...]
    o_ref[...] = (y * (1.0 + sc_ref[...]) + sh_ref[...]).astype(o_ref.dtype)


def _norm_mod(x, gain, mod, layer, scale_chunk, shift_chunk):
    tm = 512
    return pl.pallas_call(
        _norm_mod_body,
        grid=(N_TOK // tm,),
        in_specs=[pl.BlockSpec((tm, D_MODEL), lambda i: (i, 0)),
                  pl.BlockSpec((None, 1, D_MODEL), lambda i: (layer, 0, 0)),
                  _mod_spec(layer, scale_chunk, tm),
                  _mod_spec(layer, shift_chunk, tm)],
        out_specs=pl.BlockSpec((tm, D_MODEL), lambda i: (i, 0)),
        out_shape=jax.ShapeDtypeStruct((N_TOK, D_MODEL), BF16),
        compiler_params=_cparams(1),
        name="norm_mod",
    )(x, gain.reshape(DEPTH, 1, D_MODEL), mod, mod)


def _in_proj_body(h_ref, w_ref, o_ref):
    o_ref[...] = _dot(h_ref[...], w_ref[...].astype(BF16))


def _in_proj(h, w_in, layer):
    tm, tn = 2048, 512
    return pl.pallas_call(
        _in_proj_body,
        grid=(N_TOK // tm, pl.cdiv(W_IN, tn)),
        in_specs=[pl.BlockSpec((tm, D_MODEL), lambda i, j: (i, 0)),
                  pl.BlockSpec((None, D_MODEL, tn), lambda i, j: (layer, 0, j))],
        out_specs=pl.BlockSpec((tm, tn), lambda i, j: (i, j)),
        out_shape=jax.ShapeDtypeStruct((N_TOK, W_IN), F32),
        compiler_params=_cparams(2),
        name="in_proj",
    )(h, w_in)


def _head_norm(x, g, split):
    x2 = x * x
    if split:
        lo = lax.broadcasted_iota(jnp.int32, x.shape, 1) < (HEAD // 2)
        s_lo = jnp.sum(jnp.where(lo, x2, 0.0), axis=-1, keepdims=True)
        s_hi = jnp.sum(jnp.where(lo, 0.0, x2), axis=-1, keepdims=True)
        ms = jnp.where(lo, s_lo, s_hi) * (2.0 / HEAD)
    else:
        ms = jnp.mean(x2, axis=-1, keepdims=True)
    return (x * lax.rsqrt(ms + EPS)) * g


def _rope(x, cos, sin_signed):
    even = (lax.broadcasted_iota(jnp.int32, x.shape, 1) % 2) == 0
    swapped = jnp.where(even, pltpu.roll(x, HEAD - 1, axis=1), pltpu.roll(x, 1, axis=1))
    return x * cos + swapped * sin_signed


def _softmax_pieces(scores):
    m = functools.reduce(jnp.maximum, [jnp.max(s, axis=-1, keepdims=True) for s in scores])
    es = [jnp.exp(s - m) for s in scores]
    den = functools.reduce(lambda a, b: a + b, [jnp.sum(e, axis=-1, keepdims=True) for e in es])
    inv = 1.0 / den
    return [e * inv for e in es]


def _attn_body(*refs, nq, diff, rope, cache, emit_kv, aliased, tq, n_sub, lam_init):
    it = iter(refs)
    q_ref, k_ref, v_ref, qg_ref, kg_ref = (next(it) for _ in range(5))
    cos_ref = sin_ref = ck_ref = cv_ref = lam_ref = sub_ref = None
    if rope:
        cos_ref, sin_ref = next(it), next(it)
    if cache:
        ck_ref, cv_ref = next(it), next(it)
    if diff:
        lam_ref, sub_ref = next(it), next(it)
    if aliased:
        next(it)
    o_ref = next(it)

    kn = _head_norm(k_ref[...], kg_ref[...], diff)
    if emit_kv:
        kn_ref, vo_ref = next(it), next(it)
        kn_ref[...] = kn
        vo_ref[...] = v_ref[...]
    if rope:
        kn = _rope(kn, cos_ref[...], sin_ref[...])
    k_pieces = [kn.astype(BF16)]
    v_pieces = [v_ref[...].astype(BF16)]
    if cache:
        k_pieces.append(ck_ref[...].astype(BF16))
        v_pieces.append(cv_ref[...].astype(BF16))

    if diff:
        lv = lam_ref[...]
        lam = (jnp.exp(jnp.sum(lv[0:1] * lv[1:2], axis=-1, keepdims=True))
               - jnp.exp(jnp.sum(lv[2:3] * lv[3:4], axis=-1, keepdims=True)) + lam_init)
        qk_scale = (HEAD // 2) ** -0.5
    else:
        qk_scale = HEAD ** -0.5
    qg = qg_ref[...]

    def one_tile(rows, j):
        cols = slice(j * HEAD, (j + 1) * HEAD)
        qn = _head_norm(q_ref[rows, cols], qg, diff)
        if rope:
            qn = _rope(qn, cos_ref[rows, :], sin_ref[rows, :])
        if diff:
            lo = lax.broadcasted_iota(jnp.int32, qn.shape, 1) < (HEAD // 2)
            q1 = (jnp.where(lo, qn, 0.0) * qk_scale).astype(BF16)
            q2 = (jnp.where(lo, 0.0, qn) * qk_scale).astype(BF16)
            p1 = _softmax_pieces([_dot_nt(q1, kp) for kp in k_pieces])
            p2 = _softmax_pieces([_dot_nt(q2, kp) for kp in k_pieces])
            probs = [a - lam * b for a, b in zip(p1, p2)]
        else:
            qb = qn.astype(BF16)
            probs = _softmax_pieces([_dot_nt(qb, kp) * qk_scale for kp in k_pieces])
        o = functools.reduce(lambda a, b: a + b,
                             [_dot(p.astype(BF16), vp) for p, vp in zip(probs, v_pieces)])
        if diff:
            o = (_rms(o) * sub_ref[...]) * (1.0 - lam_init)
        o_ref[rows, cols] = o.astype(o_ref.dtype)

    for j in range(nq):
        if n_sub == 1:
            one_tile(slice(None), j)
        else:
            def sub_body(i, carry, j=j):
                one_tile(pl.ds(pl.multiple_of(i * tq, tq), tq), j)
                return carry
            lax.fori_loop(0, n_sub, sub_body, 0)


def _attention(proj, o_prev, *, layer, latent, diff, q_gain, k_gain, rope_tabs=None,
               cache_k=None, cache_v=None, lam_vecs=None, subln=None, lam_init=0.0):
    t = DEC_SEQ if latent else SEQ
    nb = DEC_BATCH if latent else BATCH
    row0 = N_CTX // t if latent else 0
    if diff:
        nq, nkv = 1, DIFF_HEADS
        qcol = lambda h: h
        kcol = lambda h: DIFF_HEADS + h
        vcol = lambda h: 2 * DIFF_HEADS + h
    else:
        nq, nkv = GQA_GROUP, GQA_KV_HEADS
        base = 3 * DIFF_HEADS
        qcol = lambda g: base // GQA_GROUP + g
        kcol = lambda g: base + GQA_HEADS + g
        vcol = lambda g: base + GQA_HEADS + GQA_KV_HEADS + g
    tq = 256
    n_sub = t // tq

    in_specs = [pl.BlockSpec((t, nq * HEAD), lambda b, h: (row0 + b, qcol(h))),
                pl.BlockSpec((t, HEAD), lambda b, h: (row0 + b, kcol(h))),
                pl.BlockSpec((t, HEAD), lambda b, h: (row0 + b, vcol(h))),
                pl.BlockSpec((1, HEAD), lambda b, h: (0, 0)),
                pl.BlockSpec((1, HEAD), lambda b, h: (0, 0))]
    args = [proj, proj, proj, q_gain, k_gain]
    if latent:
        in_specs += [pl.BlockSpec((t, HEAD), lambda b, h: (0, 0))] * 2
        args += list(rope_tabs)
        cspec = pl.BlockSpec((None, None, None, PAST_LEN, HEAD), lambda b, h: (b, layer, h, 0, 0))
        in_specs += [cspec, cspec]
        args += [cache_k, cache_v]
    if diff:
        in_specs += [pl.BlockSpec((4, HEAD // 2), lambda b, h: (0, 0)),
                     pl.BlockSpec((1, HEAD), lambda b, h: (0, 0))]
        args += [lam_vecs, subln]
    aliases = {}
    if latent:
        in_specs.append(pl.BlockSpec(memory_space=pl.ANY))
        aliases = {len(args): 0}
        args.append(o_prev)

    out_specs = [pl.BlockSpec((t, nq * HEAD), lambda b, h: (row0 + b, h))]
    out_shape = [jax.ShapeDtypeStruct((N_TOK, BRANCH_WIDTH), BF16)]
    if not latent:
        kv_spec = pl.BlockSpec((None, None, t, HEAD), lambda b, h: (b, h, 0, 0))
        out_specs += [kv_spec, kv_spec]
        out_shape += [jax.ShapeDtypeStruct((nb, nkv, t, HEAD), F32)] * 2

    body = functools.partial(_attn_body, nq=nq, diff=diff, rope=latent, cache=latent,
                             emit_kv=not latent, aliased=latent, tq=tq, n_sub=n_sub,
                             lam_init=lam_init)
    name = ("diff" if diff else "gqa") + ("_lat" if latent else "_ctx")
    return pl.pallas_call(
        body, grid=(nb, nkv), in_specs=in_specs, out_specs=out_specs, out_shape=out_shape,
        input_output_aliases=aliases, compiler_params=_cparams(2), name=name,
    )(*args)


def _log_sigmoid(x):
    return jnp.minimum(x, 0.0) - jnp.log1p(jnp.exp(-jnp.abs(x)))


def _seg_bcast(x, width, pos, needed):
    n = x.shape[0]
    if width >= 8:
        g = x.reshape(n // width, width, x.shape[1])[:, pos:pos + 1, :]
        return jnp.broadcast_to(g, (n // width, width, x.shape[1])).reshape(x.shape)
    off = lax.broadcasted_iota(jnp.int32, (n, 1), 0) % width
    y = None
    for j in needed:
        cand = x if j == pos else pltpu.roll(x, (j - pos) % n, axis=0)
        y = cand if y is None else jnp.where(off == j, cand, y)
    return y


def _gla_block(q, k, a_f, a_b, mask_ref):
    n = q.shape[0]
    row = lax.broadcasted_iota(jnp.int32, (n, 1), 0)
    pf, qf, pb, qb = a_f, a_f, a_b, a_b
    acc = _dot_nt(q.astype(BF16), k.astype(BF16)) * mask_ref[0]
    half, level = 1, 1
    while half < n:
        width = 2 * half
        right = (row % width) >= half
        eq = jnp.where(right, pf, qb)
        ek = jnp.where(right, pb - a_b, qf - a_f)
        s = _dot_nt((q * jnp.exp(eq)).astype(BF16), (k * jnp.exp(ek)).astype(BF16))
        acc = acc + s * mask_ref[level]
        lo_pos, hi_pos = list(range(half)), list(range(half, width))
        pf = pf + jnp.where(right, _seg_bcast(pf, width, half - 1, hi_pos), 0.0)
        pb = pb + jnp.where(right, _seg_bcast(pb, width, half - 1, hi_pos), 0.0)
        qf = qf + jnp.where(right, 0.0, _seg_bcast(qf, width, half, lo_pos))
        qb = qb + jnp.where(right, 0.0, _seg_bcast(qb, width, half, lo_pos))
        half, level = width, level + 1
    return acc, pf, qf, pb, qb


def _gla_body(*refs, n_blocks, latent):
    it = iter(refs)
    q_ref, k_ref, v_ref, r_ref, z_ref, wa_ref, ba_ref, gn_ref, mask_ref = (next(it) for _ in range(9))
    if latent:
        s0f_ref, s0b_ref = next(it), next(it)
        next(it)
    o_ref = next(it)
    if not latent:
        sf_ref, sb_ref = next(it), next(it)
    n = GLA_BLOCK
    wa = wa_ref[...].astype(BF16)
    ba = ba_ref[...]

    blocks = []
    for c in range(n_blocks):
        rows = slice(c * n, (c + 1) * n)
        z = z_ref[rows, :].astype(BF16)
        a_f = _log_sigmoid(_dot(z, wa[0]) + ba[0:1, :]) * (1.0 / GLA_TAU)
        a_b = _log_sigmoid(_dot(z, wa[1]) + ba[1:2, :]) * (1.0 / GLA_TAU)
        q = q_ref[rows, :] * (GLA_DK ** -0.5)
        k = k_ref[rows, :]
        vb = v_ref[rows, :].astype(BF16)
        acc, pf, qf, pb, qb = _gla_block(q, k, a_f, a_b, mask_ref)
        o = _dot(acc.astype(BF16), vb)
        kf = (k * jnp.exp(qf - a_f)).T.astype(BF16)
        kb = (k * jnp.exp(pb - a_b)).T.astype(BF16)
        blocks.append(dict(o=o, u_f=_dot(kf, vb), u_b=_dot(kb, vb),
                           q_f=(q * jnp.exp(pf)).astype(BF16), q_b=(q * jnp.exp(qb)).astype(BF16),
                           tot_f=pf.T[:, n - 1:n], tot_b=qb.T[:, 0:1]))

    if latent:
        s = s0f_ref[...]
        for c in range(n_blocks):
            blk = blocks[c]
            blk["o"] = blk["o"] + _dot(blk["q_f"], s.astype(BF16))
            s = jnp.exp(blk["tot_f"]) * s + blk["u_f"]
        s = s0b_ref[...]
        for c in reversed(range(n_blocks)):
            blk = blocks[c]
            blk["o"] = blk["o"] + _dot(blk["q_b"], s.astype(BF16))
            s = jnp.exp(blk["tot_b"]) * s + blk["u_b"]
    else:
        sf_ref[...] = blocks[0]["u_f"]
        sb_ref[...] = blocks[0]["u_b"]

    for c in range(n_blocks):
        rows = slice(c * n, (c + 1) * n)
        r = r_ref[rows, :]
        out = (_rms(blocks[c]["o"]) * gn_ref[...]) * (r * jax.nn.sigmoid(r))
        o_ref[rows, :] = out.astype(o_ref.dtype)


def _gla_masks():
    n = GLA_BLOCK
    t = jnp.arange(n)[:, None]
    s = jnp.arange(n)[None, :]
    masks = [2.0 * (t == s)]
    half = 1
    while half < n:
        width = 2 * half
        masks.append((t // width == s // width) & ((t % width >= half) != (s % width >= half)))
        half = width
    return jnp.stack([m.astype(F32) for m in masks])


def _gla(proj, la, wa_pad, ba, gnorm, masks, o_prev, *, layer, latent, s0_f=None, s0_b=None):
    t = DEC_SEQ if latent else SEQ
    nb = DEC_BATCH if latent else BATCH
    row0 = N_CTX // t if latent else 0
    qc, kc = 36, 40
    vc, rc = 22, 26
    n_lvl = masks.shape[0]
    in_specs = [pl.BlockSpec((t, GLA_DK), lambda b, h: (row0 + b, qc + h)),
                pl.BlockSpec((t, GLA_DK), lambda b, h: (row0 + b, kc + h)),
                pl.BlockSpec((t, GLA_DV), lambda b, h: (row0 + b, vc + h)),
                pl.BlockSpec((t, GLA_DV), lambda b, h: (row0 + b, rc + h)),
                pl.BlockSpec((t, 2 * GLA_GATE_RANK), lambda b, h: (row0 + b, 0)),
                pl.BlockSpec((None, 2, 2 * GLA_GATE_RANK, GLA_DK), lambda b, h: (layer, 0, 0, h)),
                pl.BlockSpec((None, 2, GLA_DK), lambda b, h: (layer, 0, h)),
                pl.BlockSpec((None, 1, GLA_DV), lambda b, h: (layer, 0, 0)),
                pl.BlockSpec((n_lvl, GLA_BLOCK, GLA_BLOCK), lambda b, h: (0, 0, 0))]
    args = [proj, proj, proj, proj, la, wa_pad, ba, gnorm.reshape(DEPTH, 1, GLA_DV), masks]
    aliases = {}
    if latent:
        sspec = pl.BlockSpec((None, None, None, GLA_DK, GLA_DV), lambda b, h: (b, layer, h, 0, 0))
        in_specs += [sspec, sspec, pl.BlockSpec(memory_space=pl.ANY)]
        aliases = {len(args) + 2: 0}
        args += [s0_f, s0_b, o_prev]
    out_specs = [pl.BlockSpec((t, GLA_DV), lambda b, h: (row0 + b, h))]
    out_shape = [jax.ShapeDtypeStruct((N_TOK, BRANCH_WIDTH), BF16)]
    if not latent:
        st_spec = pl.BlockSpec((None, None, GLA_DK, GLA_DV), lambda b, h: (b, h, 0, 0))
        out_specs += [st_spec, st_spec]
        out_shape += [jax.ShapeDtypeStruct((nb, GLA_HEADS, GLA_DK, GLA_DV), F32)] * 2
    body = functools.partial(_gla_body, n_blocks=t // GLA_BLOCK, latent=latent)
    return pl.pallas_call(
        body, grid=(nb, GLA_HEADS), in_specs=in_specs, out_specs=out_specs, out_shape=out_shape,
        input_output_aliases=aliases, compiler_params=_cparams(2),
        name="gla_lat" if latent else "gla_ctx",
    )(*args)


def _merge_body(h_ref, od_ref, og_ref, ol_ref, wg0_ref, wg1_ref, wg2_ref, b0_ref, b1_ref, b2_ref,
                wb_ref, o_ref):
    h = h_ref[...]
    acc = None
    for n, (br_ref, wg_ref, b_ref) in enumerate(((od_ref, wg0_ref, b0_ref), (og_ref, wg1_ref, b1_ref),
                                                 (ol_ref, wg2_ref, b2_ref))):
        gate = jax.nn.sigmoid(_dot(h, wg_ref[...].astype(BF16)) + b_ref[...])
        term = gate * _dot(br_ref[...], wb_ref[n].astype(BF16))
        acc = term if acc is None else acc + term
    o_ref[...] = acc.astype(o_ref.dtype)


def _merge(h, o_diff, o_gqa, o_gla, w_gate, b_gate, w_branch, layer):
    tm, tn = 1024, 256
    nj = D_MODEL // tn
    b3 = b_gate.reshape(DEPTH, 1, 3 * D_MODEL)
    br_spec = pl.BlockSpec((tm, BRANCH_WIDTH), lambda i, j: (i, 0))
    wg_specs = [pl.BlockSpec((None, D_MODEL, tn), lambda i, j, n=n: (layer, 0, n * nj + j)) for n in range(3)]
    b_specs = [pl.BlockSpec((None, 1, tn), lambda i, j, n=n: (layer, 0, n * nj + j)) for n in range(3)]
    return pl.pallas_call(
        _merge_body,
        grid=(N_TOK // tm, nj),
        in_specs=[pl.BlockSpec((tm, D_MODEL), lambda i, j: (i, 0)), br_spec, br_spec, br_spec,
                  *wg_specs, *b_specs,
                  pl.BlockSpec((None, 3, BRANCH_WIDTH, tn), lambda i, j: (layer, 0, 0, j))],
        out_specs=pl.BlockSpec((tm, tn), lambda i, j: (i, j)),
        out_shape=jax.ShapeDtypeStruct((N_TOK, D_MODEL), BF16),
        compiler_params=_cparams(2),
        name="merge",
    )(h, o_diff, o_gqa, o_gla, w_gate, w_gate, w_gate, b3, b3, b3, w_branch)


def _out_proj_body(m_ref, w_ref, x_ref, g_ref, o_ref):
    o_ref[...] = x_ref[...] + g_ref[...] * _dot(m_ref[...], w_ref[...].astype(BF16))


def _out_proj(m, w_out, x, mod, layer):
    tm, tn = 1024, 512
    return pl.pallas_call(
        _out_proj_body,
        grid=(N_TOK // tm, D_MODEL // tn),
        in_specs=[pl.BlockSpec((tm, D_MODEL), lambda i, j: (i, 0)),
                  pl.BlockSpec((None, D_MODEL, tn), lambda i, j: (layer, 0, j)),
                  pl.BlockSpec((tm, tn), lambda i, j: (i, j)),
                  pl.BlockSpec((None, None, 1, tn),
                               lambda i, j: (layer, _cond_row(i * tm), 0, 2 * (D_MODEL // tn) + j))],
        out_specs=pl.BlockSpec((tm, tn), lambda i, j: (i, j)),
        out_shape=jax.ShapeDtypeStruct((N_TOK, D_MODEL), F32),
        compiler_params=_cparams(2),
        name="out_proj",
    )(m, w_out, x, mod)


def _route_body(x_ref, g_ref, sc_ref, sh_ref, wr_ref, h_ref, rank_ref, aff_ref, *, t):
    y = _rms(x_ref[...]) * g_ref[...]
    hb = (y * (1.0 + sc_ref[...]) + sh_ref[...]).astype(BF16)
    h_ref[...] = hb
    logits = _dot_nt(wr_ref[...].astype(BF16), hb)
    e = jnp.exp(logits - jnp.max(logits, axis=0, keepdims=True))
    aff_t = e / jnp.sum(e, axis=0, keepdims=True)
    aff_ref[...] = aff_t
    aff = aff_t.T
    earlier = (lax.broadcasted_iota(jnp.int32, (t, t), 0) < lax.broadcasted_iota(jnp.int32, (t, t), 1))
    for ex in range(N_EXPERTS):
        col = aff[:, ex:ex + 1]
        row = aff_t[ex:ex + 1, :]
        beats = (col > row) | ((col == row) & earlier)
        rank_ref[ex:ex + 1, :] = jnp.sum(beats.astype(F32), axis=0, keepdims=True)


def _route(x, gain, mod, w_router_t, *, layer, latent):
    t = DEC_SEQ if latent else SEQ
    nb = DEC_BATCH if latent else BATCH
    row0 = N_CTX // t if latent else 0
    et_spec = pl.BlockSpec((None, N_EXPERTS, t), lambda b: (b, 0, 0))
    return pl.pallas_call(
        functools.partial(_route_body, t=t),
        grid=(nb,),
        in_specs=[pl.BlockSpec((t, D_MODEL), lambda b: (row0 + b, 0)),
                  pl.BlockSpec((None, 1, D_MODEL), lambda b: (layer, 0, 0)),
                  pl.BlockSpec((None, None, 1, D_MODEL), lambda b: (layer, _cond_row((row0 + b) * t), 0, 4)),
                  pl.BlockSpec((None, None, 1, D_MODEL), lambda b: (layer, _cond_row((row0 + b) * t), 0, 3)),
                  pl.BlockSpec((None, N_EXPERTS, D_MODEL), lambda b: (layer, 0, 0))],
        out_specs=[pl.BlockSpec((t, D_MODEL), lambda b: (b, 0)), et_spec, et_spec],
        out_shape=[jax.ShapeDtypeStruct((nb * t, D_MODEL), BF16),
                   jax.ShapeDtypeStruct((nb, N_EXPERTS, t), F32),
                   jax.ShapeDtypeStruct((nb, N_EXPERTS, t), F32)],
        compiler_params=_cparams(1),
        name="route_lat" if latent else "route_ctx",
    )(x, gain.reshape(DEPTH, 1, D_MODEL), mod, mod, w_router_t)


def _gather_body(h_ref, rank_ref, aff_ref, xg_ref, st_ref, ws_ref, *, t, cap):
    hb = h_ref[...]
    slot = lax.broadcasted_iota(jnp.int32, (cap, t), 0).astype(F32)
    sel_all = []
    for ex in range(N_EXPERTS):
        sel = (rank_ref[ex:ex + 1, :] == slot).astype(F32)
        xg_ref[ex] = _dot(sel.astype(BF16), hb).astype(BF16)
        ws_ref[ex] = jnp.sum(sel * aff_ref[ex:ex + 1, :], axis=1, keepdims=True)
        if cap % HEAD == 0:
            st_ref[:, ex * cap:(ex + 1) * cap] = sel.T.astype(BF16)
        else:
            sel_all.append(sel)
    if sel_all:
        st_ref[...] = jnp.concatenate(sel_all, axis=0).T.astype(BF16)


def _gather(hb, rank, aff, *, latent):
    t = DEC_SEQ if latent else SEQ
    nb = DEC_BATCH if latent else BATCH
    cap = 2 * t // N_EXPERTS
    et_spec = pl.BlockSpec((None, N_EXPERTS, t), lambda b: (b, 0, 0))
    return pl.pallas_call(
        functools.partial(_gather_body, t=t, cap=cap),
        grid=(nb,),
        in_specs=[pl.BlockSpec((t, D_MODEL), lambda b: (b, 0)), et_spec, et_spec],
        out_specs=[pl.BlockSpec((N_EXPERTS, cap, D_MODEL), lambda b: (0, b, 0)),
                   pl.BlockSpec((t, N_EXPERTS * cap), lambda b: (b, 0)),
                   pl.BlockSpec((N_EXPERTS, cap, 1), lambda b: (0, b, 0))],
        out_shape=[jax.ShapeDtypeStruct((N_EXPERTS, nb * cap, D_MODEL), BF16),
                   jax.ShapeDtypeStruct((nb * t, N_EXPERTS * cap), BF16),
                   jax.ShapeDtypeStruct((N_EXPERTS, nb * cap, 1), F32)],
        compiler_params=_cparams(1),
        name="gather_lat" if latent else "gather_ctx",
    )(hb, rank, aff)


def _experts_body(xc_ref, xl_ref, wsc_ref, wsl_ref, wg_ref, wu_ref, wd_ref, o_ref, acc_ref, *, nf, half):
    f = pl.program_id(1)

    @pl.when(f == 0)
    def _():
        acc_ref[...] = jnp.zeros_like(acc_ref)

    wg = wg_ref[...].astype(BF16)
    wu = wu_ref[...].astype(BF16)
    wd = wd_ref[...].astype(BF16)
    for s, x_ref in enumerate((xc_ref, xl_ref)):
        x = x_ref[...]
        g = _dot(x, wg)
        hid = (g * jax.nn.sigmoid(g)) * _dot(x, wu)
        acc_ref[s * half:(s + 1) * half, :] += _dot(hid.astype(BF16), wd)

    @pl.when(f == nf - 1)
    def _():
        o_ref[0:half, :] = (acc_ref[0:half, :] * wsc_ref[...]).astype(o_ref.dtype)
        o_ref[half:, :] = (acc_ref[half:, :] * wsl_ref[...]).astype(o_ref.dtype)


def _experts(xg_ctx, xg_lat, ws_ctx, ws_lat, w_gate, w_up, w_down, layer):
    half = xg_ctx.shape[1]
    tf = 256
    nf = D_MODEL // tf
    x_spec = pl.BlockSpec((None, half, D_MODEL), lambda e, f: (e, 0, 0))
    ws_spec = pl.BlockSpec((None, half, 1), lambda e, f: (e, 0, 0))
    return pl.pallas_call(
        functools.partial(_experts_body, nf=nf, half=half),
        grid=(N_EXPERTS, nf),
        in_specs=[x_spec, x_spec, ws_spec, ws_spec,
                  pl.BlockSpec((None, None, D_MODEL, tf), lambda e, f: (layer, e, 0, f)),
                  pl.BlockSpec((None, None, D_MODEL, tf), lambda e, f: (layer, e, 0, f)),
                  pl.BlockSpec((None, None, tf, D_MODEL), lambda e, f: (layer, e, f, 0))],
        out_specs=pl.BlockSpec((None, 2 * half, D_MODEL), lambda e, f: (e, 0, 0)),
        out_shape=jax.ShapeDtypeStruct((N_EXPERTS, 2 * half, D_MODEL), BF16),
        scratch_shapes=[pltpu.VMEM((2 * half, D_MODEL), F32)],
        compiler_params=_cparams(2),
        name="experts",
    )(xg_ctx, xg_lat, ws_ctx, ws_lat, w_gate, w_up, w_down)


def _combine_body(st_ref, ye_ref, x_ref, g_ref, o_ref, *, cap):
    ye = ye_ref[...].reshape(N_EXPERTS * cap, ye_ref.shape[-1])
    o_ref[...] = x_ref[...] + g_ref[...] * _dot(st_ref[...], ye)


def _combine(st, ye, x, mod, *, layer, latent):
    t = DEC_SEQ if latent else SEQ
    nb = DEC_BATCH if latent else BATCH
    cap = 2 * t // N_EXPERTS
    row0 = N_CTX // t if latent else 0
    slot0 = (BATCH * 2 * SEQ // N_EXPERTS) // cap if latent else 0
    tn = 512
    nj = D_MODEL // tn
    return pl.pallas_call(
        functools.partial(_combine_body, cap=cap),
        grid=(nb, nj),
        in_specs=[pl.BlockSpec((t, N_EXPERTS * cap), lambda b, j: (b, 0)),
                  pl.BlockSpec((N_EXPERTS, cap, tn), lambda b, j: (0, slot0 + b, j)),
                  pl.BlockSpec((t, tn), lambda b, j: (row0 + b, j)),
                  pl.BlockSpec((None, None, 1, tn),
                               lambda b, j: (layer, _cond_row((row0 + b) * t), 0, 5 * nj + j))],
        out_specs=pl.BlockSpec((t, tn), lambda b, j: (row0 + b, j)),
        out_shape=jax.ShapeDtypeStruct((N_TOK, D_MODEL), F32),
        input_output_aliases={2: 0},
        compiler_params=_cparams(2),
        name="combine_lat" if latent else "combine_ctx",
    )(st, ye, x, mod)


def _rope_tables(t, dim):
    rows = t // GRID_W
    row = jnp.repeat(jnp.arange(rows), GRID_W)
    col = jnp.tile(jnp.arange(GRID_W), rows)

    def angles(pos, d):
        inv = ROPE_THETA ** (-jnp.arange(0, d, 2, dtype=F32) / d)
        return pos.astype(F32)[:, None] * inv[None, :]

    ang = jnp.concatenate([angles(row, dim // 2), angles(col, dim // 2)], axis=-1)
    cos = jnp.repeat(jnp.cos(ang), 2, axis=-1)
    sin = jnp.repeat(jnp.sin(ang), 2, axis=-1) * jnp.tile(jnp.array([-1.0, 1.0], F32), dim // 2)
    reps = HEAD // dim
    return jnp.tile(cos, (1, reps)), jnp.tile(sin, (1, reps))


def kernel(x_prompt, x_sample, cache_diff_k, cache_diff_v, cache_gqa_k, cache_gqa_v, state_gla_fwd,
           state_gla_bwd, c, c_ctx, w_mod, b_mod, norm_mix, w_in, diff_q_norm, diff_k_norm,
           diff_lambda, diff_subln, gqa_q_norm, gqa_k_norm, gla_wa2, gla_ba, gla_norm, w_branch,
           w_gate, b_gate, w_out, norm_ffn, w_router, w_e_gate, w_e_up, w_e_down):
    x = jnp.concatenate([x_prompt.reshape(N_CTX, D_MODEL), x_sample.reshape(N_LAT, D_MODEL)], axis=0)
    cond = jnp.concatenate([c_ctx[None, :], c, jnp.zeros((COND_ROWS - 1 - DEC_BATCH, D_MODEL), F32)], axis=0)
    mod = _modulation(cond, w_mod, b_mod).reshape(DEPTH, COND_ROWS, 1, N_MOD * D_MODEL)

    rope_d = _rope_tables(DEC_SEQ, HEAD // 2)
    rope_g = _rope_tables(DEC_SEQ, HEAD)
    masks = _gla_masks()
    zeros = jnp.zeros((DEPTH, GLA_GATE_RANK, GLA_HEADS * GLA_DK), F32)
    wa_pad = jnp.stack([jnp.concatenate([gla_wa2[:, 0], zeros], axis=1),
                        jnp.concatenate([zeros, gla_wa2[:, 1]], axis=1)], axis=1)
    dq_gain = jnp.tile(diff_q_norm, (1, 2))
    dk_gain = jnp.tile(diff_k_norm, (1, 2))
    w_router_t = jnp.swapaxes(w_router, 1, 2)

    new = [[] for _ in range(6)]
    for l in range(DEPTH):
        lam_init = 0.8 - 0.6 * math.exp(-0.3 * l)
        h = _norm_mod(x, norm_mix, mod, l, 1, 0)
        proj = _in_proj(h, w_in, l)
        la = proj[:, LA_COL:]

        dkw = dict(layer=l, diff=True, q_gain=dq_gain[l:l + 1], k_gain=dk_gain[l:l + 1],
                   lam_vecs=diff_lambda[l], subln=diff_subln[l:l + 1], lam_init=lam_init)
        o_diff, dk_new, dv_new = _attention(proj, None, latent=False, **dkw)
        o_diff = _attention(proj, o_diff, latent=True, rope_tabs=rope_d, cache_k=cache_diff_k,
                            cache_v=cache_diff_v, **dkw)[0]
        gkw = dict(layer=l, diff=False, q_gain=gqa_q_norm[l:l + 1], k_gain=gqa_k_norm[l:l + 1])
        o_gqa, gk_new, gv_new = _attention(proj, None, latent=False, **gkw)
        o_gqa = _attention(proj, o_gqa, latent=True, rope_tabs=rope_g, cache_k=cache_gqa_k,
                           cache_v=cache_gqa_v, **gkw)[0]
        o_gla, sf_new, sb_new = _gla(proj, la, wa_pad, gla_ba, gla_norm, masks, None, layer=l, latent=False)
        o_gla = _gla(proj, la, wa_pad, gla_ba, gla_norm, masks, o_gla, layer=l, latent=True,
                     s0_f=state_gla_fwd, s0_b=state_gla_bwd)[0]
        for lst, val in zip(new, (dk_new, dv_new, gk_new, gv_new, sf_new, sb_new)):
            lst.append(val)

        m = _merge(h, o_diff, o_gqa, o_gla, w_gate, b_gate, w_branch, l)
        x = _out_proj(m, w_out, x, mod, l)

        routed = []
        for latent in (False, True):
            hb, rank, aff = _route(x, norm_ffn, mod, w_router_t, layer=l, latent=latent)
            routed.append(_gather(hb, rank, aff, latent=latent))
        (xg_c, st_c, ws_c), (xg_l, st_l, ws_l) = routed
        ye = _experts(xg_c, xg_l, ws_c, ws_l, w_e_gate, w_e_up, w_e_down, l)
        x = _combine(st_c, ye, x, mod, layer=l, latent=False)
        x = _combine(st_l, ye, x, mod, layer=l, latent=True)

    y_prompt = x[:N_CTX].reshape(BATCH, SEQ, D_MODEL)
    y_sample = x[N_CTX:].reshape(DEC_BATCH, DEC_SEQ, D_MODEL)
    return (y_prompt, y_sample) + tuple(jnp.stack(lst, axis=1) for lst in new)
```

```python
import functools
import math

import jax
import jax.numpy as jnp
from jax import lax
from jax.experimental import pallas as pl
from jax.experimental.pallas import tpu as pltpu

F32 = jnp.float32
BF16 = jnp.bfloat16

D_MODEL = 2048
BATCH = 16
SEQ = 256
DEPTH = 2
DEC_BATCH = 4
DEC_SEQ = 1024
PAST_LEN = 512
GRID_W = 64
BRANCH_WIDTH = 1024
HEAD = 128
DIFF_HEADS = 8
GQA_HEADS = 8
GQA_KV_HEADS = 2
GQA_GROUP = GQA_HEADS // GQA_KV_HEADS
GLA_HEADS = 4
GLA_DK = 128
GLA_DV = 256
GLA_GATE_RANK = 16
GLA_TAU = 16.0
GLA_BLOCK = 256
N_EXPERTS = 16
N_MOD = 6
ROPE_THETA = 10000.0
EPS = 1e-6

N_CTX = BATCH * SEQ
N_LAT = DEC_BATCH * DEC_SEQ
N_TOK = N_CTX + N_LAT
W_IN = 7712
DIFF_COL = 0
GQA_COL = 3 * DIFF_HEADS * HEAD
LA_COL = 7680
COND_ROWS = 8

VMEM_LIMIT = 56 * 1024 * 1024


def _cparams(n_axes):
    return pltpu.CompilerParams(dimension_semantics=("arbitrary",) * n_axes,
                                vmem_limit_bytes=VMEM_LIMIT)


def _dot(a, b):
    return jnp.dot(a, b, preferred_element_type=F32)


def _dot_nt(a, b):
    return lax.dot_general(a, b, (((1,), (1,)), ((), ())), preferred_element_type=F32)


def _rms(x):
    return x * lax.rsqrt(jnp.mean(x * x, axis=-1, keepdims=True) + EPS)


def _cond_row(start):
    return jnp.where(start < N_CTX, 0, 1 + (start - N_CTX) // DEC_SEQ)


def _mod_spec(layer, chunk, tm):
    return pl.BlockSpec((None, None, 1, D_MODEL),
                        lambda i, *_: (layer, _cond_row(i * tm), 0, chunk))


def _stream_specs(block, n_ctx_blocks, n_col_blocks=1):
    last_j = n_col_blocks - 1

    def ctx_map(i, *rest):
        j = rest[0] if n_col_blocks > 1 else 0
        own = i < n_ctx_blocks
        return (jnp.minimum(i, n_ctx_blocks - 1), jnp.where(own, j, last_j))

    def lat_map(i, *rest):
        j = rest[0] if n_col_blocks > 1 else 0
        own = i >= n_ctx_blocks
        return (jnp.maximum(i - n_ctx_blocks, 0), jnp.where(own, j, 0))

    return [pl.BlockSpec(block, ctx_map), pl.BlockSpec(block, lat_map)]


def _pick_stream(xc_ref, xl_ref, n_ctx_blocks):
    own_ctx = pl.program_id(0) < n_ctx_blocks
    return jnp.where(own_ctx, xc_ref[...], xl_ref[...])


def _mod_body(c_ref, w_ref, b_ref, o_ref):
    c = c_ref[...]
    a = (c * jax.nn.sigmoid(c)).astype(BF16)
    o_ref[...] = _dot(a, w_ref[...].astype(BF16)) + b_ref[...]


def _modulation(cond, w_mod, b_mod):
    n = N_MOD * D_MODEL
    tn = 1024
    return pl.pallas_call(
        _mod_body,
        grid=(DEPTH, n // tn),
        in_specs=[pl.BlockSpec((COND_ROWS, D_MODEL), lambda l, j: (0, 0)),
                  pl.BlockSpec((None, D_MODEL, tn), lambda l, j: (l, 0, j)),
                  pl.BlockSpec((None, 1, tn), lambda l, j: (l, 0, j))],
        out_specs=pl.BlockSpec((None, COND_ROWS, tn), lambda l, j: (l, 0, j)),
        out_shape=jax.ShapeDtypeStruct((DEPTH, COND_ROWS, n), F32),
        compiler_params=_cparams(2),
        name="ada_mod",
    )(cond, w_mod, b_mod.reshape(DEPTH, 1, n))


def _norm_mod_body(*refs, n_ctx_blocks):
    *x_refs, g_ref, sc_ref, sh_ref, o_ref = refs
    x = x_refs[0][...] if len(x_refs) == 1 else _pick_stream(*x_refs, n_ctx_blocks)
    y = _rms(x) * g_ref[...]
    o_ref[...] = (y * (1.0 + sc_ref[...]) + sh_ref[...]).astype(o_ref.dtype)


def _norm_mod(xs, gain, mod, layer, scale_chunk, shift_chunk):
    tm = 512
    if len(xs) == 1:
        x_specs = [pl.BlockSpec((tm, D_MODEL), lambda i: (i, 0))]
    else:
        x_specs = _stream_specs((tm, D_MODEL), N_CTX // tm)
    return pl.pallas_call(
        functools.partial(_norm_mod_body, n_ctx_blocks=N_CTX // tm),
        grid=(N_TOK // tm,),
        in_specs=[*x_specs,
                  pl.BlockSpec((None, 1, D_MODEL), lambda i: (layer, 0, 0)),
                  _mod_spec(layer, scale_chunk, tm),
                  _mod_spec(layer, shift_chunk, tm)],
        out_specs=pl.BlockSpec((tm, D_MODEL), lambda i: (i, 0)),
        out_shape=jax.ShapeDtypeStruct((N_TOK, D_MODEL), BF16),
        compiler_params=_cparams(1),
        name="norm_mod",
    )(*xs, gain.reshape(DEPTH, 1, D_MODEL), mod, mod)


def _in_proj_body(h_ref, wt_ref, o_ref):
    o_ref[...] = _dot_nt(h_ref[...], wt_ref[...].astype(BF16))


def _in_proj(h, w_in_t, layer):
    tm, tn = 2048, 512
    return pl.pallas_call(
        _in_proj_body,
        grid=(N_TOK // tm, pl.cdiv(W_IN, tn)),
        in_specs=[pl.BlockSpec((tm, D_MODEL), lambda i, j: (i, 0)),
                  pl.BlockSpec((None, tn, D_MODEL), lambda i, j: (layer, j, 0))],
        out_specs=pl.BlockSpec((tm, tn), lambda i, j: (i, j)),
        out_shape=jax.ShapeDtypeStruct((N_TOK, W_IN), F32),
        compiler_params=_cparams(2),
        name="in_proj",
    )(h, w_in_t)


def _head_norm(x, g, split):
    x2 = x * x
    if split:
        lo = lax.broadcasted_iota(jnp.int32, x.shape, 1) < (HEAD // 2)
        s_lo = jnp.sum(jnp.where(lo, x2, 0.0), axis=-1, keepdims=True)
        s_hi = jnp.sum(jnp.where(lo, 0.0, x2), axis=-1, keepdims=True)
        ms = jnp.where(lo, s_lo, s_hi) * (2.0 / HEAD)
    else:
        ms = jnp.mean(x2, axis=-1, keepdims=True)
    return (x * lax.rsqrt(ms + EPS)) * g


def _rope(x, cos, sin_signed):
    even = (lax.broadcasted_iota(jnp.int32, x.shape, 1) % 2) == 0
    swapped = jnp.where(even, pltpu.roll(x, HEAD - 1, axis=1), pltpu.roll(x, 1, axis=1))
    return x * cos + swapped * sin_signed


def _with_ones(v):
    return jnp.concatenate([v.astype(BF16), jnp.ones(v.shape, BF16)], axis=1)


def _softmax_pv(qb, k_pieces, vext_pieces):
    scores = [_dot_nt(qb, kp) for kp in k_pieces]
    m = functools.reduce(jnp.maximum, [jnp.max(s, axis=-1, keepdims=True) for s in scores])
    acc = functools.reduce(lambda a, b: a + b,
                           [_dot(jnp.exp(s - m).astype(BF16), ve) for s, ve in zip(scores, vext_pieces)])
    return acc[:, :HEAD] / acc[:, HEAD:]


def _attn_body(*refs, nkv, nq, diff, latent, stacked, tq, n_sub, lam_init):
    it = iter(refs)
    q_ref, k_ref, v_ref, qg_ref, kg_ref = (next(it) for _ in range(5))
    if latent:
        cos_ref, sin_ref, ck_ref, cv_ref = (next(it) for _ in range(4))
    if diff:
        lam_ref, sub_ref = next(it), next(it)
    if latent:
        next(it)
    if stacked:
        kprev_ref, vprev_ref = next(it), next(it)
    o_ref = next(it)
    if not latent:
        kn_ref, vo_ref = next(it), next(it)

    if diff:
        lv = lam_ref[...]
        lam = (jnp.exp(jnp.sum(lv[0:1] * lv[1:2], axis=-1, keepdims=True))
               - jnp.exp(jnp.sum(lv[2:3] * lv[3:4], axis=-1, keepdims=True)) + lam_init)
        qk_scale = (HEAD // 2) ** -0.5
    else:
        qk_scale = HEAD ** -0.5
    qg = qg_ref[...]

    for kv in range(nkv):
        kcols = slice(kv * HEAD, (kv + 1) * HEAD)
        v = v_ref[:, kcols]
        kn = _head_norm(k_ref[:, kcols], kg_ref[...], diff)
        if not latent:
            if stacked:
                kn_ref[0, kv] = kprev_ref[kv]
                vo_ref[0, kv] = vprev_ref[kv]
                kn_ref[1, kv] = kn
                vo_ref[1, kv] = v
            else:
                kn_ref[kv] = kn
                vo_ref[kv] = v
        if latent:
            kn = _rope(kn, cos_ref[...], sin_ref[...])
        k_pieces = [kn.astype(BF16)]
        vext_pieces = [_with_ones(v)]
        if latent:
            k_pieces.append(ck_ref[kv].astype(BF16))
            vext_pieces.append(_with_ones(cv_ref[kv]))

        def one_tile(rows, j, k_pieces=k_pieces, vext_pieces=vext_pieces, kv=kv):
            cols = slice((kv * nq + j) * HEAD, (kv * nq + j + 1) * HEAD)
            qn = _head_norm(q_ref[rows, cols], qg, diff)
            if latent:
                qn = _rope(qn, cos_ref[rows, :], sin_ref[rows, :])
            qn = qn * qk_scale
            if diff:
                lo = lax.broadcasted_iota(jnp.int32, qn.shape, 1) < (HEAD // 2)
                o1 = _softmax_pv(jnp.where(lo, qn, 0.0).astype(BF16), k_pieces, vext_pieces)
                o2 = _softmax_pv(jnp.where(lo, 0.0, qn).astype(BF16), k_pieces, vext_pieces)
                o = (_rms(o1 - lam * o2) * sub_ref[...]) * (1.0 - lam_init)
            else:
                o = _softmax_pv(qn.astype(BF16), k_pieces, vext_pieces)
            o_ref[rows, cols] = o.astype(o_ref.dtype)

        for j in range(nq):
            for i in range(n_sub):
                one_tile(slice(i * tq, (i + 1) * tq), j)


def _attention(proj, o_prev, *, layer, latent, diff, q_gain, k_gain, rope_tabs=None,
               cache_k=None, cache_v=None, lam_vecs=None, subln=None, lam_init=0.0, prev_kv=None):
    t = DEC_SEQ if latent else SEQ
    nb = DEC_BATCH if latent else BATCH
    row0 = N_CTX // t if latent else 0
    nq, nkv_all, col0 = (1, DIFF_HEADS, DIFF_COL) if diff else (GQA_GROUP, GQA_KV_HEADS, GQA_COL)
    nkv = 1 if latent else nkv_all
    wq, wk = nkv * nq * HEAD, nkv * HEAD
    qblk = col0 // wq
    kblk = (col0 + nkv_all * nq * HEAD) // wk
    vblk = (col0 + nkv_all * (nq + 1) * HEAD) // wk
    tq = 256
    n_sub = t // tq
    grid = (nb, nkv_all) if latent else (nb,)
    hix = (lambda g: g[1]) if latent else (lambda g: 0)

    def const(*shape):
        return pl.BlockSpec(shape, lambda *g: (0,) * len(shape))

    in_specs = [pl.BlockSpec((t, wq), lambda *g: (row0 + g[0], qblk + hix(g))),
                pl.BlockSpec((t, wk), lambda *g: (row0 + g[0], kblk + hix(g))),
                pl.BlockSpec((t, wk), lambda *g: (row0 + g[0], vblk + hix(g))),
                const(1, HEAD), const(1, HEAD)]
    args = [proj, proj, proj, q_gain, k_gain]
    if latent:
        cspec = pl.BlockSpec((None, None, 1, PAST_LEN, HEAD), lambda *g: (g[0], layer, g[1], 0, 0))
        in_specs += [const(t, HEAD), const(t, HEAD), cspec, cspec]
        args += [*rope_tabs, cache_k, cache_v]
    if diff:
        in_specs += [const(4, HEAD // 2), const(1, HEAD)]
        args += [lam_vecs, subln]
    aliases = {}
    if latent:
        in_specs.append(pl.BlockSpec(memory_space=pl.ANY))
        aliases = {len(args): 0}
        args.append(o_prev)
    stacked = prev_kv is not None
    if stacked:
        pspec = pl.BlockSpec((None, nkv, t, HEAD), lambda *g: (g[0], 0, 0, 0))
        in_specs += [pspec, pspec]
        args += list(prev_kv)

    out_specs = [pl.BlockSpec((t, wq), lambda *g: (row0 + g[0], hix(g)))]
    out_shape = [jax.ShapeDtypeStruct((N_TOK, BRANCH_WIDTH), BF16)]
    if not latent:
        if stacked:
            kv_spec = pl.BlockSpec((DEPTH, None, nkv, t, HEAD), lambda *g: (0, g[0], 0, 0, 0))
            kv_shape = jax.ShapeDtypeStruct((DEPTH, nb, nkv, t, HEAD), F32)
        else:
            kv_spec = pl.BlockSpec((None, nkv, t, HEAD), lambda *g: (g[0], 0, 0, 0))
            kv_shape = jax.ShapeDtypeStruct((nb, nkv, t, HEAD), F32)
        out_specs += [kv_spec, kv_spec]
        out_shape += [kv_shape, kv_shape]

    body = functools.partial(_attn_body, nkv=nkv, nq=nq, diff=diff, latent=latent, stacked=stacked,
                             tq=tq, n_sub=n_sub, lam_init=lam_init)
    name = ("diff" if diff else "gqa") + ("_lat" if latent else "_ctx")
    return pl.pallas_call(
        body, grid=grid, in_specs=in_specs, out_specs=out_specs, out_shape=out_shape,
        input_output_aliases=aliases, compiler_params=_cparams(len(grid)), name=name,
    )(*args)


def _log_sigmoid(x):
    return jnp.minimum(x, 0.0) - jnp.log1p(jnp.exp(-jnp.abs(x)))


def _seg_bcast(x, width, pos, needed):
    n = x.shape[0]
    if width >= 8:
        g = x.reshape(n // width, width, x.shape[1])[:, pos:pos + 1, :]
        return jnp.broadcast_to(g, (n // width, width, x.shape[1])).reshape(x.shape)
    off = lax.broadcasted_iota(jnp.int32, (n, 1), 0) % width
    y = None
    for j in needed:
        cand = x if j == pos else pltpu.roll(x, (j - pos) % n, axis=0)
        y = cand if y is None else jnp.where(off == j, cand, y)
    return y


def _gla_block(q, k, a_f, a_b, mask_ref):
    n = q.shape[0]
    row = lax.broadcasted_iota(jnp.int32, (n, 1), 0)
    pf, qf, pb, qb = a_f, a_f, a_b, a_b
    acc = _dot_nt(q.astype(BF16), k.astype(BF16)) * mask_ref[0]
    half, level = 1, 1
    while half < n:
        width = 2 * half
        right = (row % width) >= half
        eq = jnp.where(right, pf, qb)
        ek = jnp.where(right, pb - a_b, qf - a_f)
        s = _dot_nt((q * jnp.exp(eq)).astype(BF16), (k * jnp.exp(ek)).astype(BF16))
        acc = acc + s * mask_ref[level]
        lo_pos, hi_pos = list(range(half)), list(range(half, width))
        pf = pf + jnp.where(right, _seg_bcast(pf, width, half - 1, hi_pos), 0.0)
        pb = pb + jnp.where(right, _seg_bcast(pb, width, half - 1, hi_pos), 0.0)
        qf = qf + jnp.where(right, 0.0, _seg_bcast(qf, width, half, lo_pos))
        qb = qb + jnp.where(right, 0.0, _seg_bcast(qb, width, half, lo_pos))
        half, level = width, level + 1
    return acc, pf, qf, pb, qb


def _gla_body(*refs, n_blocks, latent, stacked):
    it = iter(refs)
    q_ref, k_ref, v_ref, r_ref, z_ref, wa_ref, ba_ref, gn_ref, mask_ref = (next(it) for _ in range(9))
    if latent:
        s0f_ref, s0b_ref = next(it), next(it)
        next(it)
    if stacked:
        sfprev_ref, sbprev_ref = next(it), next(it)
    o_ref = next(it)
    if not latent:
        sf_ref, sb_ref = next(it), next(it)
    n = GLA_BLOCK
    wa = wa_ref[...].astype(BF16)
    ba = ba_ref[...]

    blocks = []
    for c in range(n_blocks):
        rows = slice(c * n, (c + 1) * n)
        z = z_ref[rows, :].astype(BF16)
        a_f = _log_sigmoid(_dot(z, wa[0]) + ba[0:1, :]) * (1.0 / GLA_TAU)
        a_b = _log_sigmoid(_dot(z, wa[1]) + ba[1:2, :]) * (1.0 / GLA_TAU)
        q = q_ref[rows, :] * (GLA_DK ** -0.5)
        k = k_ref[rows, :]
        vb = v_ref[rows, :].astype(BF16)
        acc, pf, qf, pb, qb = _gla_block(q, k, a_f, a_b, mask_ref)
        o = _dot(acc.astype(BF16), vb)
        kf = (k * jnp.exp(qf - a_f)).T.astype(BF16)
        kb = (k * jnp.exp(pb - a_b)).T.astype(BF16)
        blocks.append(dict(o=o, u_f=_dot(kf, vb), u_b=_dot(kb, vb),
                           q_f=(q * jnp.exp(pf)).astype(BF16), q_b=(q * jnp.exp(qb)).astype(BF16),
                           tot_f=pf.T[:, n - 1:n], tot_b=qb.T[:, 0:1]))

    if latent:
        s = s0f_ref[...]
        for c in range(n_blocks):
            blk = blocks[c]
            blk["o"] = blk["o"] + _dot(blk["q_f"], s.astype(BF16))
            s = jnp.exp(blk["tot_f"]) * s + blk["u_f"]
        s = s0b_ref[...]
        for c in reversed(range(n_blocks)):
            blk = blocks[c]
            blk["o"] = blk["o"] + _dot(blk["q_b"], s.astype(BF16))
            s = jnp.exp(blk["tot_b"]) * s + blk["u_b"]
    elif stacked:
        sf_ref[0] = sfprev_ref[...]
        sb_ref[0] = sbprev_ref[...]
        sf_ref[1] = blocks[0]["u_f"]
        sb_ref[1] = blocks[0]["u_b"]
    else:
        sf_ref[...] = blocks[0]["u_f"]
        sb_ref[...] = blocks[0]["u_b"]

    for c in range(n_blocks):
        rows = slice(c * n, (c + 1) * n)
        r = r_ref[rows, :]
        out = (_rms(blocks[c]["o"]) * gn_ref[...]) * (r * jax.nn.sigmoid(r))
        o_ref[rows, :] = out.astype(o_ref.dtype)


def _gla_masks():
    n = GLA_BLOCK
    t = jnp.arange(n)[:, None]
    s = jnp.arange(n)[None, :]
    masks = [2.0 * (t == s)]
    half = 1
    while half < n:
        width = 2 * half
        masks.append((t // width == s // width) & ((t % width >= half) != (s % width >= half)))
        half = width
    return jnp.stack([m.astype(F32) for m in masks])


def _gla(proj, la, wa_pad, ba, gnorm, masks, o_prev, *, layer, latent, s0_f=None, s0_b=None,
         prev_states=None):
    t = DEC_SEQ if latent else SEQ
    nb = DEC_BATCH if latent else BATCH
    row0 = N_CTX // t if latent else 0
    qc, kc = 36, 40
    vc, rc = 22, 26
    n_lvl = masks.shape[0]
    in_specs = [pl.BlockSpec((t, GLA_DK), lambda b, h: (row0 + b, qc + h)),
                pl.BlockSpec((t, GLA_DK), lambda b, h: (row0 + b, kc + h)),
                pl.BlockSpec((t, GLA_DV), lambda b, h: (row0 + b, vc + h)),
                pl.BlockSpec((t, GLA_DV), lambda b, h: (row0 + b, rc + h)),
                pl.BlockSpec((t, 2 * GLA_GATE_RANK), lambda b, h: (row0 + b, 0)),
                pl.BlockSpec((None, 2, 2 * GLA_GATE_RANK, GLA_DK), lambda b, h: (layer, 0, 0, h)),
                pl.BlockSpec((None, 2, GLA_DK), lambda b, h: (layer, 0, h)),
                pl.BlockSpec((None, 1, GLA_DV), lambda b, h: (layer, 0, 0)),
                pl.BlockSpec((n_lvl, GLA_BLOCK, GLA_BLOCK), lambda b, h: (0, 0, 0))]
    args = [proj, proj, proj, proj, la, wa_pad, ba, gnorm.reshape(DEPTH, 1, GLA_DV), masks]
    aliases = {}
    if latent:
        sspec = pl.BlockSpec((None, None, None, GLA_DK, GLA_DV), lambda b, h: (b, layer, h, 0, 0))
        in_specs += [sspec, sspec, pl.BlockSpec(memory_space=pl.ANY)]
        aliases = {len(args) + 2: 0}
        args += [s0_f, s0_b, o_prev]
    stacked = prev_states is not None
    if stacked:
        pspec = pl.BlockSpec((None, None, GLA_DK, GLA_DV), lambda b, h: (b, h, 0, 0))
        in_specs += [pspec, pspec]
        args += list(prev_states)
    out_specs = [pl.BlockSpec((t, GLA_DV), lambda b, h: (row0 + b, h))]
    out_shape = [jax.ShapeDtypeStruct((N_TOK, BRANCH_WIDTH), BF16)]
    if not latent:
        if stacked:
            st_spec = pl.BlockSpec((DEPTH, None, None, GLA_DK, GLA_DV), lambda b, h: (0, b, h, 0, 0))
            st_shape = jax.ShapeDtypeStruct((DEPTH, nb, GLA_HEADS, GLA_DK, GLA_DV), F32)
        else:
            st_spec = pl.BlockSpec((None, None, GLA_DK, GLA_DV), lambda b, h: (b, h, 0, 0))
            st_shape = jax.ShapeDtypeStruct((nb, GLA_HEADS, GLA_DK, GLA_DV), F32)
        out_specs += [st_spec, st_spec]
        out_shape += [st_shape, st_shape]
    body = functools.partial(_gla_body, n_blocks=t // GLA_BLOCK, latent=latent, stacked=stacked)
    return pl.pallas_call(
        body, grid=(nb, GLA_HEADS), in_specs=in_specs, out_specs=out_specs, out_shape=out_shape,
        input_output_aliases=aliases, compiler_params=_cparams(2),
        name="gla_lat" if latent else "gla_ctx",
    )(*args)


def _merge_body(h_ref, od_ref, og_ref, ol_ref, wg0_ref, wg1_ref, wg2_ref, b0_ref, b1_ref, b2_ref,
                wb_ref, o_ref):
    h = h_ref[...]
    acc = None
    for n, (br_ref, wg_ref, b_ref) in enumerate(((od_ref, wg0_ref, b0_ref), (og_ref, wg1_ref, b1_ref),
                                                 (ol_ref, wg2_ref, b2_ref))):
        gate = jax.nn.sigmoid(_dot(h, wg_ref[...].astype(BF16)) + b_ref[...])
        term = gate * _dot(br_ref[...], wb_ref[n].astype(BF16))
        acc = term if acc is None else acc + term
    o_ref[...] = acc.astype(o_ref.dtype)


def _merge(h, o_diff, o_gqa, o_gla, w_gate, b_gate, w_branch, layer):
    tm, tn = 1024, 256
    nj = D_MODEL // tn
    b3 = b_gate.reshape(DEPTH, 1, 3 * D_MODEL)
    br_spec = pl.BlockSpec((tm, BRANCH_WIDTH), lambda i, j: (i, 0))
    wg_specs = [pl.BlockSpec((None, D_MODEL, tn), lambda i, j, n=n: (layer, 0, n * nj + j)) for n in range(3)]
    b_specs = [pl.BlockSpec((None, 1, tn), lambda i, j, n=n: (layer, 0, n * nj + j)) for n in range(3)]
    return pl.pallas_call(
        _merge_body,
        grid=(N_TOK // tm, nj),
        in_specs=[pl.BlockSpec((tm, D_MODEL), lambda i, j: (i, 0)), br_spec, br_spec, br_spec,
                  *wg_specs, *b_specs,
                  pl.BlockSpec((None, 3, BRANCH_WIDTH, tn), lambda i, j: (layer, 0, 0, j))],
        out_specs=pl.BlockSpec((tm, tn), lambda i, j: (i, j)),
        out_shape=jax.ShapeDtypeStruct((N_TOK, D_MODEL), BF16),
        compiler_params=_cparams(2),
        name="merge",
    )(h, o_diff, o_gqa, o_gla, w_gate, w_gate, w_gate, b3, b3, b3, w_branch)


def _out_proj_body(*refs, n_ctx_blocks):
    m_ref, w_ref, *x_refs, g_ref, o_ref = refs
    x = x_refs[0][...] if len(x_refs) == 1 else _pick_stream(*x_refs, n_ctx_blocks)
    o_ref[...] = x + g_ref[...] * _dot(m_ref[...], w_ref[...].astype(BF16))


def _out_proj(m, w_out, xs, mod, layer):
    tm, tn = 1024, 512
    nj = D_MODEL // tn
    if len(xs) == 1:
        x_specs = [pl.BlockSpec((tm, tn), lambda i, j: (i, j))]
    else:
        x_specs = _stream_specs((tm, tn), N_CTX // tm, nj)
    return pl.pallas_call(
        functools.partial(_out_proj_body, n_ctx_blocks=N_CTX // tm),
        grid=(N_TOK // tm, nj),
        in_specs=[pl.BlockSpec((tm, D_MODEL), lambda i, j: (i, 0)),
                  pl.BlockSpec((None, D_MODEL, tn), lambda i, j: (layer, 0, j)),
                  *x_specs,
                  pl.BlockSpec((None, None, 1, tn),
                               lambda i, j: (layer, _cond_row(i * tm), 0, 2 * nj + j))],
        out_specs=pl.BlockSpec((tm, tn), lambda i, j: (i, j)),
        out_shape=jax.ShapeDtypeStruct((N_TOK, D_MODEL), F32),
        compiler_params=_cparams(2),
        name="out_proj",
    )(m, w_out, *xs, mod)


def _route_body(x_ref, g_ref, sc_ref, sh_ref, wr_ref, h_ref, rank_ref, aff_ref, *, t):
    y = _rms(x_ref[...]) * g_ref[...]
    hb = (y * (1.0 + sc_ref[...]) + sh_ref[...]).astype(BF16)
    h_ref[...] = hb
    logits = _dot_nt(wr_ref[...].astype(BF16), hb)
    e = jnp.exp(logits - jnp.max(logits, axis=0, keepdims=True))
    aff_t = e / jnp.sum(e, axis=0, keepdims=True)
    aff_ref[...] = aff_t
    aff = aff_t.T
    earlier = (lax.broadcasted_iota(jnp.int32, (t, t), 0) < lax.broadcasted_iota(jnp.int32, (t, t), 1))
    for ex in range(N_EXPERTS):
        col = aff[:, ex:ex + 1]
        row = aff_t[ex:ex + 1, :]
        beats = (col > row) | ((col == row) & earlier)
        rank_ref[ex:ex + 1, :] = jnp.sum(beats.astype(F32), axis=0, keepdims=True)


def _route(x, gain, mod, w_router_t, *, layer, latent):
    t = DEC_SEQ if latent else SEQ
    nb = DEC_BATCH if latent else BATCH
    row0 = N_CTX // t if latent else 0
    et_spec = pl.BlockSpec((None, N_EXPERTS, t), lambda b: (b, 0, 0))
    return pl.pallas_call(
        functools.partial(_route_body, t=t),
        grid=(nb,),
        in_specs=[pl.BlockSpec((t, D_MODEL), lambda b: (row0 + b, 0)),
                  pl.BlockSpec((None, 1, D_MODEL), lambda b: (layer, 0, 0)),
                  pl.BlockSpec((None, None, 1, D_MODEL), lambda b: (layer, _cond_row((row0 + b) * t), 0, 4)),
                  pl.BlockSpec((None, None, 1, D_MODEL), lambda b: (layer, _cond_row((row0 + b) * t), 0, 3)),
                  pl.BlockSpec((None, N_EXPERTS, D_MODEL), lambda b: (layer, 0, 0))],
        out_specs=[pl.BlockSpec((t, D_MODEL), lambda b: (b, 0)), et_spec, et_spec],
        out_shape=[jax.ShapeDtypeStruct((nb * t, D_MODEL), BF16),
                   jax.ShapeDtypeStruct((nb, N_EXPERTS, t), F32),
                   jax.ShapeDtypeStruct((nb, N_EXPERTS, t), F32)],
        compiler_params=_cparams(1),
        name="route_lat" if latent else "route_ctx",
    )(x, gain.reshape(DEPTH, 1, D_MODEL), mod, mod, w_router_t)


def _gather_body(h_ref, rank_ref, aff_ref, xg_ref, st_ref, ws_ref, *, t, cap, group):
    hb = h_ref[...]
    slot = lax.broadcasted_iota(jnp.int32, (cap, t), 0).astype(F32)
    for g0 in range(0, N_EXPERTS, group):
        sels = []
        for ex in range(g0, g0 + group):
            sel = (rank_ref[ex:ex + 1, :] == slot).astype(F32)
            ws_ref[ex] = jnp.sum(sel * aff_ref[ex:ex + 1, :], axis=1, keepdims=True)
            sels.append(sel)
        sel_g = jnp.concatenate(sels, axis=0)
        xg = _dot(sel_g.astype(BF16), hb).astype(BF16)
        xg_ref[g0:g0 + group] = xg.reshape(group, cap, D_MODEL)
        st_ref[:, g0 * cap:(g0 + group) * cap] = sel_g.T.astype(BF16)


def _gather(hb, rank, aff, *, latent):
    t = DEC_SEQ if latent else SEQ
    nb = DEC_BATCH if latent else BATCH
    cap = 2 * t // N_EXPERTS
    group = 512 // cap
    et_spec = pl.BlockSpec((None, N_EXPERTS, t), lambda b: (b, 0, 0))
    return pl.pallas_call(
        functools.partial(_gather_body, t=t, cap=cap, group=group),
        grid=(nb,),
        in_specs=[pl.BlockSpec((t, D_MODEL), lambda b: (b, 0)), et_spec, et_spec],
        out_specs=[pl.BlockSpec((N_EXPERTS, cap, D_MODEL), lambda b: (0, b, 0)),
                   pl.BlockSpec((t, N_EXPERTS * cap), lambda b: (b, 0)),
                   pl.BlockSpec((N_EXPERTS, cap, 1), lambda b: (0, b, 0))],
        out_shape=[jax.ShapeDtypeStruct((N_EXPERTS, nb * cap, D_MODEL), BF16),
                   jax.ShapeDtypeStruct((nb * t, N_EXPERTS * cap), BF16),
                   jax.ShapeDtypeStruct((N_EXPERTS, nb * cap, 1), F32)],
        compiler_params=_cparams(1),
        name="gather_lat" if latent else "gather_ctx",
    )(hb, rank, aff)


def _experts_body(xc_ref, xl_ref, wsc_ref, wsl_ref, wg_ref, wu_ref, wd_ref, o_ref, acc_ref, *, nf, half):
    f = pl.program_id(1)

    @pl.when(f == 0)
    def _():
        acc_ref[...] = jnp.zeros_like(acc_ref)

    wg = wg_ref[...].astype(BF16)
    wu = wu_ref[...].astype(BF16)
    wd = wd_ref[...].astype(BF16)
    for s, x_ref in enumerate((xc_ref, xl_ref)):
        x = x_ref[...]
        g = _dot(x, wg)
        hid = (g * jax.nn.sigmoid(g)) * _dot(x, wu)
        acc_ref[s * half:(s + 1) * half, :] += _dot(hid.astype(BF16), wd)

    @pl.when(f == nf - 1)
    def _():
        o_ref[0:half, :] = (acc_ref[0:half, :] * wsc_ref[...]).astype(o_ref.dtype)
        o_ref[half:, :] = (acc_ref[half:, :] * wsl_ref[...]).astype(o_ref.dtype)


def _experts(xg_ctx, xg_lat, ws_ctx, ws_lat, w_gate, w_up, w_down, layer):
    half = xg_ctx.shape[1]
    tf = 256
    nf = D_MODEL // tf
    x_spec = pl.BlockSpec((None, half, D_MODEL), lambda e, f: (e, 0, 0))
    ws_spec = pl.BlockSpec((None, half, 1), lambda e, f: (e, 0, 0))
    return pl.pallas_call(
        functools.partial(_experts_body, nf=nf, half=half),
        grid=(N_EXPERTS, nf),
        in_specs=[x_spec, x_spec, ws_spec, ws_spec,
                  pl.BlockSpec((None, None, D_MODEL, tf), lambda e, f: (layer, e, 0, f)),
                  pl.BlockSpec((None, None, D_MODEL, tf), lambda e, f: (layer, e, 0, f)),
                  pl.BlockSpec((None, None, tf, D_MODEL), lambda e, f: (layer, e, f, 0))],
        out_specs=pl.BlockSpec((None, 2 * half, D_MODEL), lambda e, f: (e, 0, 0)),
        out_shape=jax.ShapeDtypeStruct((N_EXPERTS, 2 * half, D_MODEL), BF16),
        scratch_shapes=[pltpu.VMEM((2 * half, D_MODEL), F32)],
        compiler_params=_cparams(2),
        name="experts",
    )(xg_ctx, xg_lat, ws_ctx, ws_lat, w_gate, w_up, w_down)


def _combine_body(st_ref, ye_ref, x_ref, g_ref, o_ref, *, cap):
    ye = ye_ref[...].reshape(N_EXPERTS * cap, ye_ref.shape[-1])
    o_ref[...] = x_ref[...] + g_ref[...] * _dot(st_ref[...], ye)


def _combine(st, ye, x, mod, *, layer, latent, final):
    t = DEC_SEQ if latent else SEQ
    nb = DEC_BATCH if latent else BATCH
    cap = 2 * t // N_EXPERTS
    row0 = N_CTX // t if latent else 0
    slot0 = (BATCH * 2 * SEQ // N_EXPERTS) // cap if latent else 0
    tn = 512 if latent else D_MODEL
    nj = D_MODEL // tn
    out_row0 = 0 if final else row0
    return pl.pallas_call(
        functools.partial(_combine_body, cap=cap),
        grid=(nb, nj),
        in_specs=[pl.BlockSpec((t, N_EXPERTS * cap), lambda b, j: (b, 0)),
                  pl.BlockSpec((N_EXPERTS, cap, tn), lambda b, j: (0, slot0 + b, j)),
                  pl.BlockSpec((t, tn), lambda b, j: (row0 + b, j)),
                  pl.BlockSpec((None, None, 1, tn),
                               lambda b, j: (layer, _cond_row((row0 + b) * t), 0, 5 * nj + j))],
        out_specs=pl.BlockSpec((t, tn), lambda b, j: (out_row0 + b, j)),
        out_shape=jax.ShapeDtypeStruct((nb * t if final else N_TOK, D_MODEL), F32),
        input_output_aliases={} if final else {2: 0},
        compiler_params=_cparams(2),
        name="combine_lat" if latent else "combine_ctx",
    )(st, ye, x, mod)


def _rope_tables(t, dim):
    rows = t // GRID_W
    row = jnp.repeat(jnp.arange(rows), GRID_W)
    col = jnp.tile(jnp.arange(GRID_W), rows)

    def angles(pos, d):
        inv = ROPE_THETA ** (-jnp.arange(0, d, 2, dtype=F32) / d)
        return pos.astype(F32)[:, None] * inv[None, :]

    ang = jnp.concatenate([angles(row, dim // 2), angles(col, dim // 2)], axis=-1)
    cos = jnp.repeat(jnp.cos(ang), 2, axis=-1)
    sin = jnp.repeat(jnp.sin(ang), 2, axis=-1) * jnp.tile(jnp.array([-1.0, 1.0], F32), dim // 2)
    reps = HEAD // dim
    return jnp.tile(cos, (1, reps)), jnp.tile(sin, (1, reps))


def kernel(x_prompt, x_sample, cache_diff_k, cache_diff_v, cache_gqa_k, cache_gqa_v, state_gla_fwd,
           state_gla_bwd, c, c_ctx, w_mod, b_mod, norm_mix, w_in, diff_q_norm, diff_k_norm,
           diff_lambda, diff_subln, gqa_q_norm, gqa_k_norm, gla_wa2, gla_ba, gla_norm, w_branch,
           w_gate, b_gate, w_out, norm_ffn, w_router, w_e_gate, w_e_up, w_e_down):
    xs = (x_prompt.reshape(N_CTX, D_MODEL), x_sample.reshape(N_LAT, D_MODEL))
    cond = jnp.concatenate([c_ctx[None, :], c, jnp.zeros((COND_ROWS - 1 - DEC_BATCH, D_MODEL), F32)], axis=0)
    mod = _modulation(cond, w_mod, b_mod).reshape(DEPTH, COND_ROWS, 1, N_MOD * D_MODEL)

    rope_d = _rope_tables(DEC_SEQ, HEAD // 2)
    rope_g = _rope_tables(DEC_SEQ, HEAD)
    masks = _gla_masks()
    zeros = jnp.zeros((DEPTH, GLA_GATE_RANK, GLA_HEADS * GLA_DK), F32)
    wa_pad = jnp.stack([jnp.concatenate([gla_wa2[:, 0], zeros], axis=1),
                        jnp.concatenate([zeros, gla_wa2[:, 1]], axis=1)], axis=1)
    dq_gain = jnp.tile(diff_q_norm, (1, 2))
    dk_gain = jnp.tile(diff_k_norm, (1, 2))
    w_router_t = jnp.swapaxes(w_router, 1, 2)
    w_in_t = jnp.swapaxes(w_in, 1, 2)

    diff_kv = gqa_kv = gla_st = None
    for l in range(DEPTH):
        last = l == DEPTH - 1
        lam_init = 0.8 - 0.6 * math.exp(-0.3 * l)
        h = _norm_mod(xs, norm_mix, mod, l, 1, 0)
        proj = _in_proj(h, w_in_t, l)
        la = proj[:, LA_COL:]

        dkw = dict(layer=l, diff=True, q_gain=dq_gain[l:l + 1], k_gain=dk_gain[l:l + 1],
                   lam_vecs=diff_lambda[l], subln=diff_subln[l:l + 1], lam_init=lam_init)
        o_diff, *diff_kv = _attention(proj, None, latent=False, prev_kv=diff_kv, **dkw)
        o_diff = _attention(proj, o_diff, latent=True, rope_tabs=rope_d, cache_k=cache_diff_k,
                            cache_v=cache_diff_v, **dkw)[0]
        gkw = dict(layer=l, diff=False, q_gain=gqa_q_norm[l:l + 1], k_gain=gqa_k_norm[l:l + 1])
        o_gqa, *gqa_kv = _attention(proj, None, latent=False, prev_kv=gqa_kv, **gkw)
        o_gqa = _attention(proj, o_gqa, latent=True, rope_tabs=rope_g, cache_k=cache_gqa_k,
                           cache_v=cache_gqa_v, **gkw)[0]
        o_gla, *gla_st = _gla(proj, la, wa_pad, gla_ba, gla_norm, masks, None, layer=l, latent=False,
                              prev_states=gla_st)
        o_gla = _gla(proj, la, wa_pad, gla_ba, gla_norm, masks, o_gla, layer=l, latent=True,
                     s0_f=state_gla_fwd, s0_b=state_gla_bwd)[0]

        m = _merge(h, o_diff, o_gqa, o_gla, w_gate, b_gate, w_branch, l)
        x = _out_proj(m, w_out, xs, mod, l)

        routed = []
        for latent in (False, True):
            hb, rank, aff = _route(x, norm_ffn, mod, w_router_t, layer=l, latent=latent)
            routed.append(_gather(hb, rank, aff, latent=latent))
        (xg_c, st_c, ws_c), (xg_l, st_l, ws_l) = routed
        ye = _experts(xg_c, xg_l, ws_c, ws_l, w_e_gate, w_e_up, w_e_down, l)
        if last:
            y_prompt = _combine(st_c, ye, x, mod, layer=l, latent=False, final=True)
            y_sample = _combine(st_l, ye, x, mod, layer=l, latent=True, final=True)
        else:
            x = _combine(st_c, ye, x, mod, layer=l, latent=False, final=False)
            x = _combine(st_l, ye, x, mod, layer=l, latent=True, final=False)
            xs = (x,)

    caches = tuple(jnp.swapaxes(a, 0, 1) for a in (*diff_kv, *gqa_kv, *gla_st))
    return (y_prompt.reshape(BATCH, SEQ, D_MODEL), y_sample.reshape(DEC_BATCH, DEC_SEQ, D_MODEL)) + caches
```

```python
import functools
import math

import jax
import jax.numpy as jnp
from jax import lax
from jax.experimental import pallas as pl
from jax.experimental.pallas import tpu as pltpu

F32 = jnp.float32
BF16 = jnp.bfloat16

D_MODEL = 2048
BATCH = 16
SEQ = 256
DEPTH = 2
DEC_BATCH = 4
DEC_SEQ = 1024
PAST_LEN = 512
GRID_W = 64
BRANCH_WIDTH = 1024
HEAD = 128
DIFF_HEADS = 8
GQA_HEADS = 8
GQA_KV_HEADS = 2
GQA_GROUP = GQA_HEADS // GQA_KV_HEADS
GLA_HEADS = 4
GLA_DK = 128
GLA_DV = 256
GLA_GATE_RANK = 16
GLA_TAU = 16.0
GLA_BLOCK = 256
N_EXPERTS = 16
N_MOD = 6
ROPE_THETA = 10000.0
EPS = 1e-6

N_CTX = BATCH * SEQ
N_LAT = DEC_BATCH * DEC_SEQ
N_TOK = N_CTX + N_LAT
W_IN = 7712
DIFF_COL = 0
GQA_COL = 3 * DIFF_HEADS * HEAD
LA_COL = 7680
COND_ROWS = 8

VMEM_LIMIT = 56 * 1024 * 1024


def _cparams(n_axes):
    return pltpu.CompilerParams(dimension_semantics=("arbitrary",) * n_axes,
                                vmem_limit_bytes=VMEM_LIMIT)


def _dot(a, b):
    return jnp.dot(a, b, preferred_element_type=F32)


def _dot_nt(a, b):
    return lax.dot_general(a, b, (((1,), (1,)), ((), ())), preferred_element_type=F32)


def _rms(x):
    return x * lax.rsqrt(jnp.mean(x * x, axis=-1, keepdims=True) + EPS)


def _cond_row(start):
    return jnp.where(start < N_CTX, 0, 1 + (start - N_CTX) // DEC_SEQ)


def _mod_spec(layer, chunk, tm):
    return pl.BlockSpec((None, None, 1, D_MODEL),
                        lambda i, *_: (layer, _cond_row(i * tm), 0, chunk))


def _stream_specs(block, n_ctx_blocks, n_col_blocks=1):
    last_j = n_col_blocks - 1

    def ctx_map(i, *rest):
        j = rest[0] if n_col_blocks > 1 else 0
        own = i < n_ctx_blocks
        return (jnp.minimum(i, n_ctx_blocks - 1), jnp.where(own, j, last_j))

    def lat_map(i, *rest):
        j = rest[0] if n_col_blocks > 1 else 0
        own = i >= n_ctx_blocks
        return (jnp.maximum(i - n_ctx_blocks, 0), jnp.where(own, j, 0))

    return [pl.BlockSpec(block, ctx_map), pl.BlockSpec(block, lat_map)]


def _pick_stream(xc_ref, xl_ref, n_ctx_blocks):
    own_ctx = pl.program_id(0) < n_ctx_blocks
    return jnp.where(own_ctx, xc_ref[...], xl_ref[...])


def _mod_body(c_ref, w_ref, b_ref, o_ref):
    c = c_ref[...]
    a = (c * jax.nn.sigmoid(c)).astype(BF16)
    o_ref[...] = _dot(a, w_ref[...].astype(BF16)) + b_ref[...]


def _modulation(cond, w_mod, b_mod):
    n = N_MOD * D_MODEL
    tn = 1024
    return pl.pallas_call(
        _mod_body,
        grid=(DEPTH, n // tn),
        in_specs=[pl.BlockSpec((COND_ROWS, D_MODEL), lambda l, j: (0, 0)),
                  pl.BlockSpec((None, D_MODEL, tn), lambda l, j: (l, 0, j)),
                  pl.BlockSpec((None, 1, tn), lambda l, j: (l, 0, j))],
        out_specs=pl.BlockSpec((None, COND_ROWS, tn), lambda l, j: (l, 0, j)),
        out_shape=jax.ShapeDtypeStruct((DEPTH, COND_ROWS, n), F32),
        compiler_params=_cparams(2),
        name="ada_mod",
    )(cond, w_mod, b_mod.reshape(DEPTH, 1, n))


def _norm_mod_body(*refs, n_ctx_blocks):
    *x_refs, g_ref, sc_ref, sh_ref, o_ref = refs
    x = x_refs[0][...] if len(x_refs) == 1 else _pick_stream(*x_refs, n_ctx_blocks)
    y = _rms(x) * g_ref[...]
    o_ref[...] = (y * (1.0 + sc_ref[...]) + sh_ref[...]).astype(o_ref.dtype)


def _norm_mod(xs, gain, mod, layer, scale_chunk, shift_chunk):
    tm = 512
    if len(xs) == 1:
        x_specs = [pl.BlockSpec((tm, D_MODEL), lambda i: (i, 0))]
    else:
        x_specs = _stream_specs((tm, D_MODEL), N_CTX // tm)
    return pl.pallas_call(
        functools.partial(_norm_mod_body, n_ctx_blocks=N_CTX // tm),
        grid=(N_TOK // tm,),
        in_specs=[*x_specs,
                  pl.BlockSpec((None, 1, D_MODEL), lambda i: (layer, 0, 0)),
                  _mod_spec(layer, scale_chunk, tm),
                  _mod_spec(layer, shift_chunk, tm)],
        out_specs=pl.BlockSpec((tm, D_MODEL), lambda i: (i, 0)),
        out_shape=jax.ShapeDtypeStruct((N_TOK, D_MODEL), BF16),
        compiler_params=_cparams(1),
        name="norm_mod",
    )(*xs, gain.reshape(DEPTH, 1, D_MODEL), mod, mod)


def _in_proj_body(h_ref, wt_ref, o_ref):
    o_ref[...] = _dot_nt(h_ref[...], wt_ref[...].astype(BF16)).astype(o_ref.dtype)


def _in_proj(h, w_in_t, layer):
    tm, tn = 2048, 512
    return pl.pallas_call(
        _in_proj_body,
        grid=(N_TOK // tm, pl.cdiv(W_IN, tn)),
        in_specs=[pl.BlockSpec((tm, D_MODEL), lambda i, j: (i, 0)),
                  pl.BlockSpec((None, tn, D_MODEL), lambda i, j: (layer, j, 0))],
        out_specs=pl.BlockSpec((tm, tn), lambda i, j: (i, j)),
        out_shape=jax.ShapeDtypeStruct((N_TOK, W_IN), BF16),
        compiler_params=_cparams(2),
        name="in_proj",
    )(h, w_in_t)


def _head_norm(x, g, split):
    x2 = x * x
    if split:
        lo = lax.broadcasted_iota(jnp.int32, x.shape, 1) < (HEAD // 2)
        s_lo = jnp.sum(jnp.where(lo, x2, 0.0), axis=-1, keepdims=True)
        s_hi = jnp.sum(jnp.where(lo, 0.0, x2), axis=-1, keepdims=True)
        ms = jnp.where(lo, s_lo, s_hi) * (2.0 / HEAD)
    else:
        ms = jnp.mean(x2, axis=-1, keepdims=True)
    return (x * lax.rsqrt(ms + EPS)) * g


def _rope(x, cos, sin_signed):
    even = (lax.broadcasted_iota(jnp.int32, x.shape, 1) % 2) == 0
    swapped = jnp.where(even, pltpu.roll(x, HEAD - 1, axis=1), pltpu.roll(x, 1, axis=1))
    return x * cos + swapped * sin_signed


def _with_ones(v):
    return jnp.concatenate([v.astype(BF16), jnp.ones(v.shape, BF16)], axis=1)


def _softmax_pv(qb, k_pieces, vext_pieces):
    scores = [_dot_nt(qb, kp) for kp in k_pieces]
    m = functools.reduce(jnp.maximum, [jnp.max(s, axis=-1, keepdims=True) for s in scores])
    acc = functools.reduce(lambda a, b: a + b,
                           [_dot(jnp.exp(s - m).astype(BF16), ve) for s, ve in zip(scores, vext_pieces)])
    return acc[:, :HEAD] / acc[:, HEAD:]


def _attn_body(*refs, nkv, nq, diff, latent, stacked, tq, n_sub, lam_init):
    it = iter(refs)
    q_ref, k_ref, v_ref, qg_ref, kg_ref = (next(it) for _ in range(5))
    if latent:
        cos_ref, sin_ref, ck_ref, cv_ref = (next(it) for _ in range(4))
    if diff:
        lam_ref, sub_ref = next(it), next(it)
    if stacked:
        kprev_ref, vprev_ref = next(it), next(it)
    o_ref = next(it)
    if not latent:
        kn_ref, vo_ref = next(it), next(it)

    if diff:
        lv = lam_ref[...]
        lam = (jnp.exp(jnp.sum(lv[0:1] * lv[1:2], axis=-1, keepdims=True))
               - jnp.exp(jnp.sum(lv[2:3] * lv[3:4], axis=-1, keepdims=True)) + lam_init)
        qk_scale = (HEAD // 2) ** -0.5
    else:
        qk_scale = HEAD ** -0.5
    qg = qg_ref[...]

    for kv in range(nkv):
        kcols = slice(kv * HEAD, (kv + 1) * HEAD)
        v = v_ref[:, kcols]
        kn = _head_norm(k_ref[:, kcols].astype(F32), kg_ref[...], diff)
        if not latent:
            if stacked:
                kn_ref[0, kv] = kprev_ref[kv]
                vo_ref[0, kv] = vprev_ref[kv]
                kn_ref[1, kv] = kn
                vo_ref[1, kv] = v.astype(F32)
            else:
                kn_ref[kv] = kn
                vo_ref[kv] = v.astype(F32)
        if latent:
            kn = _rope(kn, cos_ref[...], sin_ref[...])
        k_pieces = [kn.astype(BF16)]
        vext_pieces = [_with_ones(v)]
        if latent:
            k_pieces.append(ck_ref[kv].astype(BF16))
            vext_pieces.append(_with_ones(cv_ref[kv]))

        def one_tile(rows, j, k_pieces=k_pieces, vext_pieces=vext_pieces, kv=kv):
            cols = slice((kv * nq + j) * HEAD, (kv * nq + j + 1) * HEAD)
            qn = _head_norm(q_ref[rows, cols].astype(F32), qg, diff)
            if latent:
                qn = _rope(qn, cos_ref[rows, :], sin_ref[rows, :])
            qn = qn * qk_scale
            if diff:
                lo = lax.broadcasted_iota(jnp.int32, qn.shape, 1) < (HEAD // 2)
                o1 = _softmax_pv(jnp.where(lo, qn, 0.0).astype(BF16), k_pieces, vext_pieces)
                o2 = _softmax_pv(jnp.where(lo, 0.0, qn).astype(BF16), k_pieces, vext_pieces)
                o = (_rms(o1 - lam * o2) * sub_ref[...]) * (1.0 - lam_init)
            else:
                o = _softmax_pv(qn.astype(BF16), k_pieces, vext_pieces)
            o_ref[rows, cols] = o.astype(o_ref.dtype)

        for j in range(nq):
            for i in range(n_sub):
                one_tile(slice(i * tq, (i + 1) * tq), j)


def _attention(proj, *, layer, latent, diff, q_gain, k_gain, rope_tabs=None,
               cache_k=None, cache_v=None, lam_vecs=None, subln=None, lam_init=0.0, prev_kv=None):
    t = DEC_SEQ if latent else SEQ
    nb = DEC_BATCH if latent else BATCH
    row0 = N_CTX // t if latent else 0
    nq, nkv_all, col0 = (1, DIFF_HEADS, DIFF_COL) if diff else (GQA_GROUP, GQA_KV_HEADS, GQA_COL)
    nkv = 1 if latent else nkv_all
    wq, wk = nkv * nq * HEAD, nkv * HEAD
    qblk = col0 // wq
    kblk = (col0 + nkv_all * nq * HEAD) // wk
    vblk = (col0 + nkv_all * (nq + 1) * HEAD) // wk
    tq = 256
    n_sub = t // tq
    grid = (nb, nkv_all) if latent else (nb,)
    hix = (lambda g: g[1]) if latent else (lambda g: 0)

    def const(*shape):
        return pl.BlockSpec(shape, lambda *g: (0,) * len(shape))

    in_specs = [pl.BlockSpec((t, wq), lambda *g: (row0 + g[0], qblk + hix(g))),
                pl.BlockSpec((t, wk), lambda *g: (row0 + g[0], kblk + hix(g))),
                pl.BlockSpec((t, wk), lambda *g: (row0 + g[0], vblk + hix(g))),
                const(1, HEAD), const(1, HEAD)]
    args = [proj, proj, proj, q_gain, k_gain]
    if latent:
        cspec = pl.BlockSpec((None, None, 1, PAST_LEN, HEAD), lambda *g: (g[0], layer, g[1], 0, 0))
        in_specs += [const(t, HEAD), const(t, HEAD), cspec, cspec]
        args += [*rope_tabs, cache_k, cache_v]
    if diff:
        in_specs += [const(4, HEAD // 2), const(1, HEAD)]
        args += [lam_vecs, subln]
    stacked = prev_kv is not None
    if stacked:
        pspec = pl.BlockSpec((None, nkv, t, HEAD), lambda *g: (g[0], 0, 0, 0))
        in_specs += [pspec, pspec]
        args += list(prev_kv)

    out_specs = [pl.BlockSpec((t, wq), lambda *g: (g[0], hix(g)))]
    out_shape = [jax.ShapeDtypeStruct((nb * t, BRANCH_WIDTH), BF16)]
    if not latent:
        if stacked:
            kv_spec = pl.BlockSpec((DEPTH, None, nkv, t, HEAD), lambda *g: (0, g[0], 0, 0, 0))
            kv_shape = jax.ShapeDtypeStruct((DEPTH, nb, nkv, t, HEAD), F32)
        else:
            kv_spec = pl.BlockSpec((None, nkv, t, HEAD), lambda *g: (g[0], 0, 0, 0))
            kv_shape = jax.ShapeDtypeStruct((nb, nkv, t, HEAD), F32)
        out_specs += [kv_spec, kv_spec]
        out_shape += [kv_shape, kv_shape]

    body = functools.partial(_attn_body, nkv=nkv, nq=nq, diff=diff, latent=latent, stacked=stacked,
                             tq=tq, n_sub=n_sub, lam_init=lam_init)
    name = ("diff" if diff else "gqa") + ("_lat" if latent else "_ctx")
    return pl.pallas_call(
        body, grid=grid, in_specs=in_specs, out_specs=out_specs, out_shape=out_shape,
        compiler_params=_cparams(len(grid)), name=name,
    )(*args)


def _log_sigmoid(x):
    return jnp.minimum(x, 0.0) - jnp.log1p(jnp.exp(-jnp.abs(x)))


def _seg_bcast(x, width, pos, needed):
    n = x.shape[0]
    if width >= 8:
        g = x.reshape(n // width, width, x.shape[1])[:, pos:pos + 1, :]
        return jnp.broadcast_to(g, (n // width, width, x.shape[1])).reshape(x.shape)
    off = lax.broadcasted_iota(jnp.int32, (n, 1), 0) % width
    y = None
    for j in needed:
        cand = x if j == pos else pltpu.roll(x, (j - pos) % n, axis=0)
        y = cand if y is None else jnp.where(off == j, cand, y)
    return y


def _gla_block(q, k, a_f, a_b, mask_ref):
    n = q.shape[0]
    row = lax.broadcasted_iota(jnp.int32, (n, 1), 0)
    pf, qf, pb, qb = a_f, a_f, a_b, a_b
    acc = _dot_nt(q.astype(BF16), k.astype(BF16)) * mask_ref[0]
    half, level = 1, 1
    while half < n:
        width = 2 * half
        right = (row % width) >= half
        eq = jnp.where(right, pf, qb)
        ek = jnp.where(right, pb - a_b, qf - a_f)
        s = _dot_nt((q * jnp.exp(eq)).astype(BF16), (k * jnp.exp(ek)).astype(BF16))
        acc = acc + s * mask_ref[level]
        lo_pos, hi_pos = list(range(half)), list(range(half, width))
        pf = pf + jnp.where(right, _seg_bcast(pf, width, half - 1, hi_pos), 0.0)
        pb = pb + jnp.where(right, _seg_bcast(pb, width, half - 1, hi_pos), 0.0)
        qf = qf + jnp.where(right, 0.0, _seg_bcast(qf, width, half, lo_pos))
        qb = qb + jnp.where(right, 0.0, _seg_bcast(qb, width, half, lo_pos))
        half, level = width, level + 1
    return acc, pf, qf, pb, qb


def _gla_body(*refs, n_blocks, latent, stacked):
    it = iter(refs)
    q_ref, k_ref, v_ref, r_ref, z_ref, wa_ref, ba_ref, gn_ref, mask_ref = (next(it) for _ in range(9))
    if latent:
        s0f_ref, s0b_ref = next(it), next(it)
    if stacked:
        sfprev_ref, sbprev_ref = next(it), next(it)
    o_ref = next(it)
    if not latent:
        sf_ref, sb_ref = next(it), next(it)
    n = GLA_BLOCK
    wa = wa_ref[...].astype(BF16)
    ba = ba_ref[...]

    blocks = []
    for c in range(n_blocks):
        rows = slice(c * n, (c + 1) * n)
        z = z_ref[rows, :]
        a_f = _log_sigmoid(_dot(z, wa[0]) + ba[0:1, :]) * (1.0 / GLA_TAU)
        a_b = _log_sigmoid(_dot(z, wa[1]) + ba[1:2, :]) * (1.0 / GLA_TAU)
        q = q_ref[rows, :].astype(F32) * (GLA_DK ** -0.5)
        k = k_ref[rows, :].astype(F32)
        vb = v_ref[rows, :]
        acc, pf, qf, pb, qb = _gla_block(q, k, a_f, a_b, mask_ref)
        o = _dot(acc.astype(BF16), vb)
        kf = (k * jnp.exp(qf - a_f)).T.astype(BF16)
        kb = (k * jnp.exp(pb - a_b)).T.astype(BF16)
        blocks.append(dict(o=o, u_f=_dot(kf, vb), u_b=_dot(kb, vb),
                           q_f=(q * jnp.exp(pf)).astype(BF16), q_b=(q * jnp.exp(qb)).astype(BF16),
                           tot_f=pf.T[:, n - 1:n], tot_b=qb.T[:, 0:1]))

    if latent:
        s = s0f_ref[...]
        for c in range(n_blocks):
            blk = blocks[c]
            blk["o"] = blk["o"] + _dot(blk["q_f"], s.astype(BF16))
            s = jnp.exp(blk["tot_f"]) * s + blk["u_f"]
        s = s0b_ref[...]
        for c in reversed(range(n_blocks)):
            blk = blocks[c]
            blk["o"] = blk["o"] + _dot(blk["q_b"], s.astype(BF16))
            s = jnp.exp(blk["tot_b"]) * s + blk["u_b"]
    elif stacked:
        sf_ref[0] = sfprev_ref[...]
        sb_ref[0] = sbprev_ref[...]
        sf_ref[1] = blocks[0]["u_f"]
        sb_ref[1] = blocks[0]["u_b"]
    else:
        sf_ref[...] = blocks[0]["u_f"]
        sb_ref[...] = blocks[0]["u_b"]

    for c in range(n_blocks):
        rows = slice(c * n, (c + 1) * n)
        r = r_ref[rows, :].astype(F32)
        out =(_rms(blocks[c]["o"]) * gn_ref[...]) * (r * jax.nn.sigmoid(r))
        o_ref[rows, :] = out.astype(o_ref.dtype)


def _gla_masks():
    n = GLA_BLOCK
    t = jnp.arange(n)[:, None]
    s = jnp.arange(n)[None, :]
    masks = [2.0 * (t == s)]
    half = 1
    while half < n:
        width = 2 * half
        masks.append((t // width == s // width) & ((t % width >= half) != (s % width >= half)))
        half = width
    return jnp.stack([m.astype(F32) for m in masks])


def _gla(proj, la, wa_pad, ba, gnorm, masks, *, layer, latent, s0_f=None, s0_b=None,
         prev_states=None):
    t = DEC_SEQ if latent else SEQ
    nb = DEC_BATCH if latent else BATCH
    row0 = N_CTX // t if latent else 0
    qc, kc = 36, 40
    vc, rc = 22, 26
    n_lvl = masks.shape[0]
    in_specs = [pl.BlockSpec((t, GLA_DK), lambda b, h: (row0 + b, qc + h)),
                pl.BlockSpec((t, GLA_DK), lambda b, h: (row0 + b, kc + h)),
                pl.BlockSpec((t, GLA_DV), lambda b, h: (row0 + b, vc + h)),
                pl.BlockSpec((t, GLA_DV), lambda b, h: (row0 + b, rc + h)),
                pl.BlockSpec((t, 2 * GLA_GATE_RANK), lambda b, h: (row0 + b, 0)),
                pl.BlockSpec((None, 2, 2 * GLA_GATE_RANK, GLA_DK), lambda b, h: (layer, 0, 0, h)),
                pl.BlockSpec((None, 2, GLA_DK), lambda b, h: (layer, 0, h)),
                pl.BlockSpec((None, 1, GLA_DV), lambda b, h: (layer, 0, 0)),
                pl.BlockSpec((n_lvl, GLA_BLOCK, GLA_BLOCK), lambda b, h: (0, 0, 0))]
    args = [proj, proj, proj, proj, la, wa_pad, ba, gnorm.reshape(DEPTH, 1, GLA_DV), masks]
    if latent:
        sspec = pl.BlockSpec((None, None, None, GLA_DK, GLA_DV), lambda b, h: (b, layer, h, 0, 0))
        in_specs += [sspec, sspec]
        args += [s0_f, s0_b]
    stacked = prev_states is not None
    if stacked:
        pspec = pl.BlockSpec((None, None, GLA_DK, GLA_DV), lambda b, h: (b, h, 0, 0))
        in_specs += [pspec, pspec]
        args += list(prev_states)
    out_specs = [pl.BlockSpec((t, GLA_DV), lambda b, h: (b, h))]
    out_shape = [jax.ShapeDtypeStruct((nb * t, BRANCH_WIDTH), BF16)]
    if not latent:
        if stacked:
            st_spec = pl.BlockSpec((DEPTH, None, None, GLA_DK, GLA_DV), lambda b, h: (0, b, h, 0, 0))
            st_shape = jax.ShapeDtypeStruct((DEPTH, nb, GLA_HEADS, GLA_DK, GLA_DV), F32)
        else:
            st_spec = pl.BlockSpec((None, None, GLA_DK, GLA_DV), lambda b, h: (b, h, 0, 0))
            st_shape = jax.ShapeDtypeStruct((nb, GLA_HEADS, GLA_DK, GLA_DV), F32)
        out_specs += [st_spec, st_spec]
        out_shape += [st_shape, st_shape]
    body = functools.partial(_gla_body, n_blocks=t // GLA_BLOCK, latent=latent, stacked=stacked)
    return pl.pallas_call(
        body, grid=(nb, GLA_HEADS), in_specs=in_specs, out_specs=out_specs, out_shape=out_shape,
        compiler_params=_cparams(2),
        name="gla_lat" if latent else "gla_ctx",
    )(*args)


def _merge_body(h_ref, od_ref, og_ref, ol_ref, wg0_ref, wg1_ref, wg2_ref, b0_ref, b1_ref, b2_ref,
                wb_ref, o_ref):
    h = h_ref[...]
    acc = None
    for n, (br_ref, wg_ref, b_ref) in enumerate(((od_ref, wg0_ref, b0_ref), (og_ref, wg1_ref, b1_ref),
                                                 (ol_ref, wg2_ref, b2_ref))):
        gate = jax.nn.sigmoid(_dot(h, wg_ref[...].astype(BF16)) + b_ref[...])
        term = gate * _dot(br_ref[...], wb_ref[n].astype(BF16))
        acc = term if acc is None else acc + term
    o_ref[...] = acc.astype(o_ref.dtype)


def _merge(h, branches, w_gate, b_gate, w_branch, *, layer, latent):
    tm, tn = 1024, 256
    nj = D_MODEL // tn
    n_rows = N_LAT if latent else N_CTX
    row0 = N_CTX // tm if latent else 0
    b3 = b_gate.reshape(DEPTH, 1, 3 * D_MODEL)
    br_spec = pl.BlockSpec((tm, BRANCH_WIDTH), lambda i, j: (i, 0))
    wg_specs = [pl.BlockSpec((None, D_MODEL, tn), lambda i, j, n=n: (layer, 0, n * nj + j)) for n in range(3)]
    b_specs = [pl.BlockSpec((None, 1, tn), lambda i, j, n=n: (layer, 0, n * nj + j)) for n in range(3)]
    return pl.pallas_call(
        _merge_body,
        grid=(n_rows // tm, nj),
        in_specs=[pl.BlockSpec((tm, D_MODEL), lambda i, j: (row0 + i, 0)), br_spec, br_spec, br_spec,
                  *wg_specs, *b_specs,
                  pl.BlockSpec((None, 3, BRANCH_WIDTH, tn), lambda i, j: (layer, 0, 0, j))],
        out_specs=pl.BlockSpec((tm, tn), lambda i, j: (i, j)),
        out_shape=jax.ShapeDtypeStruct((n_rows, D_MODEL), BF16),
        compiler_params=_cparams(2),
        name="merge_lat" if latent else "merge_ctx",
    )(h, *branches, w_gate, w_gate, w_gate, b3, b3, b3, w_branch)


def _out_proj_body(m_ref, w_ref, x_ref, *rest):
    *g_refs, o_ref = rest
    y = _dot(m_ref[...], w_ref[...].astype(BF16))
    rows = y.shape[0] // len(g_refs)
    for k, g_ref in enumerate(g_refs):
        sl = slice(k * rows, (k + 1) * rows)
        o_ref[sl, :] = x_ref[sl, :] + g_ref[...] * y[sl, :]


def _out_proj(m, w_out, x, mod, *, layer, latent):
    tm, tn = 2048, 512
    nj = D_MODEL // tn
    n_rows = N_LAT if latent else N_CTX
    row0 = N_CTX if latent else 0
    n_gates = tm // DEC_SEQ
    g_specs = [pl.BlockSpec((None, None, 1, tn),
                            lambda i, j, k=k: (layer, _cond_row(row0 + i * tm + k * DEC_SEQ), 0, 2 * nj + j))
               for k in range(n_gates)]
    return pl.pallas_call(
        _out_proj_body,
        grid=(n_rows // tm, nj),
        in_specs=[pl.BlockSpec((tm, D_MODEL), lambda i, j: (i, 0)),
                  pl.BlockSpec((None, D_MODEL, tn), lambda i, j: (layer, 0, j)),
                  pl.BlockSpec((tm, tn), lambda i, j: (i, j)),
                  *g_specs],
        out_specs=pl.BlockSpec((tm, tn), lambda i, j: (i, j)),
        out_shape=jax.ShapeDtypeStruct((n_rows, D_MODEL), F32),
        compiler_params=_cparams(2),
        name="out_proj_lat" if latent else "out_proj_ctx",
    )(m, w_out, x, *([mod] * n_gates))


def _route_body(x_ref, g_ref, sc_ref, sh_ref, wr_ref, h_ref, rank_ref, aff_ref, *, t):
    y = _rms(x_ref[...]) * g_ref[...]
    hb = (y * (1.0 + sc_ref[...]) + sh_ref[...]).astype(BF16)
    h_ref[...] = hb
    logits = _dot_nt(wr_ref[...].astype(BF16), hb)
    e = jnp.exp(logits - jnp.max(logits, axis=0, keepdims=True))
    aff_t = e / jnp.sum(e, axis=0, keepdims=True)
    aff_ref[...] = aff_t
    aff = aff_t.T
    earlier = (lax.broadcasted_iota(jnp.int32, (t, t), 0) < lax.broadcasted_iota(jnp.int32, (t, t), 1))
    for ex in range(N_EXPERTS):
        col = aff[:, ex:ex + 1]
        row = aff_t[ex:ex + 1, :]
        beats = (col > row) | ((col == row) & earlier)
        rank_ref[ex:ex + 1, :] = jnp.sum(beats.astype(F32), axis=0, keepdims=True)


def _route(x, gain, mod, w_router_t, *, layer, latent):
    t = DEC_SEQ if latent else SEQ
    nb = DEC_BATCH if latent else BATCH
    row0 = N_CTX // t if latent else 0
    et_spec = pl.BlockSpec((None, N_EXPERTS, t), lambda b: (b, 0, 0))
    return pl.pallas_call(
        functools.partial(_route_body, t=t),
        grid=(nb,),
        in_specs=[pl.BlockSpec((t, D_MODEL), lambda b: (b, 0)),
                  pl.BlockSpec((None, 1, D_MODEL), lambda b: (layer, 0, 0)),
                  pl.BlockSpec((None, None, 1, D_MODEL), lambda b: (layer, _cond_row((row0 + b) * t), 0, 4)),
                  pl.BlockSpec((None, None, 1, D_MODEL), lambda b: (layer, _cond_row((row0 + b) * t), 0, 3)),
                  pl.BlockSpec((None, N_EXPERTS, D_MODEL), lambda b: (layer, 0, 0))],
        out_specs=[pl.BlockSpec((t, D_MODEL), lambda b: (b, 0)), et_spec, et_spec],
        out_shape=[jax.ShapeDtypeStruct((nb * t, D_MODEL), BF16),
                   jax.ShapeDtypeStruct((nb, N_EXPERTS, t), F32),
                   jax.ShapeDtypeStruct((nb, N_EXPERTS, t), F32)],
        compiler_params=_cparams(1),
        name="route_lat" if latent else "route_ctx",
    )(x, gain.reshape(DEPTH, 1, D_MODEL), mod, mod, w_router_t)


def _gather_body(h_ref, rank_ref, aff_ref, xg_ref, st_ref, ws_ref, *, t, cap, group):
    hb = h_ref[...]
    slot = lax.broadcasted_iota(jnp.int32, (cap, t), 0).astype(F32)
    for g0 in range(0, N_EXPERTS, group):
        sels = []
        for ex in range(g0, g0 + group):
            sel = (rank_ref[ex:ex + 1, :] == slot).astype(F32)
            ws_ref[ex] = jnp.sum(sel * aff_ref[ex:ex + 1, :], axis=1, keepdims=True)
            sels.append(sel)
        sel_g = jnp.concatenate(sels, axis=0)
        xg = _dot(sel_g.astype(BF16), hb).astype(BF16)
        xg_ref[g0:g0 + group] = xg.reshape(group, cap, D_MODEL)
        st_ref[:, g0 * cap:(g0 + group) * cap] = sel_g.T.astype(BF16)


def _gather(hb, rank, aff, *, latent):
    t = DEC_SEQ if latent else SEQ
    nb = DEC_BATCH if latent else BATCH
    cap = 2 * t // N_EXPERTS
    group = 512 // cap
    et_spec = pl.BlockSpec((None, N_EXPERTS, t), lambda b: (b, 0, 0))
    return pl.pallas_call(
        functools.partial(_gather_body, t=t, cap=cap, group=group),
        grid=(nb,),
        in_specs=[pl.BlockSpec((t, D_MODEL), lambda b: (b, 0)), et_spec, et_spec],
        out_specs=[pl.BlockSpec((N_EXPERTS, cap, D_MODEL), lambda b: (0, b, 0)),
                   pl.BlockSpec((t, N_EXPERTS * cap), lambda b: (b, 0)),
                   pl.BlockSpec((N_EXPERTS, cap, 1), lambda b: (0, b, 0))],
        out_shape=[jax.ShapeDtypeStruct((N_EXPERTS, nb * cap, D_MODEL), BF16),
                   jax.ShapeDtypeStruct((nb * t, N_EXPERTS * cap), BF16),
                   jax.ShapeDtypeStruct((N_EXPERTS, nb * cap, 1), F32)],
        compiler_params=_cparams(1),
        name="gather_lat" if latent else "gather_ctx",
    )(hb, rank, aff)


def _experts_body(xc_ref, xl_ref, wsc_ref, wsl_ref, wg_ref, wu_ref, wd_ref, o_ref, hid_ref, *, nf, tf, half):
    s = pl.program_id(1)

    @pl.when(s < nf)
    def _():
        wg = wg_ref[...].astype(BF16)
        wu = wu_ref[...].astype(BF16)
        for idx, x_ref in enumerate((xc_ref, xl_ref)):
            x = x_ref[...]
            g = _dot(x, wg)
            hid = (g * jax.nn.sigmoid(g)) * _dot(x, wu)
            hid_ref[s, idx * half:(idx + 1) * half, :] = hid.astype(BF16)

    @pl.when(s >= nf)
    def _():
        wd = wd_ref[...].astype(BF16)
        for idx, ws_ref in enumerate((wsc_ref, wsl_ref)):
            rows = slice(idx * half, (idx + 1) * half)
            acc = functools.reduce(lambda a, b: a + b,
                                   [_dot(hid_ref[f, rows, :], wd[f * tf:(f + 1) * tf, :]) for f in range(nf)])
            o_ref[rows, :] = (acc * ws_ref[...]).astype(o_ref.dtype)


def _experts(xg_ctx, xg_lat, ws_ctx, ws_lat, w_gate, w_up, w_down, layer):
    half = xg_ctx.shape[1]
    tf = tn = 512
    nf, nn = D_MODEL // tf, D_MODEL // tn
    x_spec = pl.BlockSpec((None, half, D_MODEL), lambda e, s: (e, 0, 0))
    ws_spec = pl.BlockSpec((None, half, 1), lambda e, s: (e, 0, 0))
    up_spec = pl.BlockSpec((None, None, D_MODEL, tf), lambda e, s: (layer, e, 0, jnp.minimum(s, nf - 1)))
    return pl.pallas_call(
        functools.partial(_experts_body, nf=nf, tf=tf, half=half),
        grid=(N_EXPERTS, nf + nn),
        in_specs=[x_spec, x_spec, ws_spec, ws_spec, up_spec, up_spec,
                  pl.BlockSpec((None, None, D_MODEL, tn), lambda e, s: (layer, e, 0, jnp.maximum(s - nf, 0)))],
        out_specs=pl.BlockSpec((None, 2 * half, tn), lambda e, s: (e, 0, jnp.maximum(s - nf, 0))),
        out_shape=jax.ShapeDtypeStruct((N_EXPERTS, 2 * half, D_MODEL), BF16),
        scratch_shapes=[pltpu.VMEM((nf, 2 * half, tf), BF16)],
        compiler_params=_cparams(2),
        name="experts",
    )(xg_ctx, xg_lat, ws_ctx, ws_lat, w_gate, w_up, w_down)


def _combine_body(st_ref, ye_ref, x_ref, g_ref, o_ref, *, cap):
    ye = ye_ref[...].reshape(N_EXPERTS * cap, ye_ref.shape[-1])
    o_ref[...] = x_ref[...] + g_ref[...] * _dot(st_ref[...], ye)


def _combine(st, ye, x, mod, *, layer, latent):
    t = DEC_SEQ if latent else SEQ
    nb = DEC_BATCH if latent else BATCH
    cap = 2 * t // N_EXPERTS
    row0 = N_CTX // t if latent else 0
    slot0 = (BATCH * 2 * SEQ // N_EXPERTS) // cap if latent else 0
    tn = 512 if latent else D_MODEL
    nj = D_MODEL // tn
    return pl.pallas_call(
        functools.partial(_combine_body, cap=cap),
        grid=(nb, nj),
        in_specs=[pl.BlockSpec((t, N_EXPERTS * cap), lambda b, j: (b, 0)),
                  pl.BlockSpec((N_EXPERTS, cap, tn), lambda b, j: (0, slot0 + b, j)),
                  pl.BlockSpec((t, tn), lambda b, j: (b, j)),
                  pl.BlockSpec((None, None, 1, tn),
                               lambda b, j: (layer, _cond_row((row0 + b) * t), 0, 5 * nj + j))],
        out_specs=pl.BlockSpec((t, tn), lambda b, j: (b, j)),
        out_shape=jax.ShapeDtypeStruct((nb * t, D_MODEL), F32),
        input_output_aliases={2: 0},
        compiler_params=_cparams(2),
        name="combine_lat" if latent else "combine_ctx",
    )(st, ye, x, mod)


def _rope_tables(t, dim):
    rows = t // GRID_W
    row = jnp.repeat(jnp.arange(rows), GRID_W)
    col = jnp.tile(jnp.arange(GRID_W), rows)

    def angles(pos, d):
        inv = ROPE_THETA ** (-jnp.arange(0, d, 2, dtype=F32) / d)
        return pos.astype(F32)[:, None] * inv[None, :]

    ang = jnp.concatenate([angles(row, dim // 2), angles(col, dim // 2)], axis=-1)
    cos = jnp.repeat(jnp.cos(ang), 2, axis=-1)
    sin = jnp.repeat(jnp.sin(ang), 2, axis=-1) * jnp.tile(jnp.array([-1.0, 1.0], F32), dim // 2)
    reps = HEAD // dim
    return jnp.tile(cos, (1, reps)), jnp.tile(sin, (1, reps))


def kernel(x_prompt, x_sample, cache_diff_k, cache_diff_v, cache_gqa_k, cache_gqa_v, state_gla_fwd,
           state_gla_bwd, c, c_ctx, w_mod, b_mod, norm_mix, w_in, diff_q_norm, diff_k_norm,
           diff_lambda, diff_subln, gqa_q_norm, gqa_k_norm, gla_wa2, gla_ba, gla_norm, w_branch,
           w_gate, b_gate, w_out, norm_ffn, w_router, w_e_gate, w_e_up, w_e_down):
    xs = (x_prompt.reshape(N_CTX, D_MODEL), x_sample.reshape(N_LAT, D_MODEL))
    cond = jnp.concatenate([c_ctx[None, :], c, jnp.zeros((COND_ROWS - 1 - DEC_BATCH, D_MODEL), F32)], axis=0)
    mod = _modulation(cond, w_mod, b_mod).reshape(DEPTH, COND_ROWS, 1, N_MOD * D_MODEL)

    rope_d = _rope_tables(DEC_SEQ, HEAD // 2)
    rope_g = _rope_tables(DEC_SEQ, HEAD)
    masks = _gla_masks()
    zeros = jnp.zeros((DEPTH, GLA_GATE_RANK, GLA_HEADS * GLA_DK), F32)
    wa_pad = jnp.stack([jnp.concatenate([gla_wa2[:, 0], zeros], axis=1),
                        jnp.concatenate([zeros, gla_wa2[:, 1]], axis=1)], axis=1)
    dq_gain = jnp.tile(diff_q_norm, (1, 2))
    dk_gain = jnp.tile(diff_k_norm, (1, 2))
    w_router_t = jnp.swapaxes(w_router, 1, 2)
    w_in_t = jnp.swapaxes(w_in, 1, 2)

    diff_kv = gqa_kv = gla_st = None
    for l in range(DEPTH):
        lam_init = 0.8 - 0.6 * math.exp(-0.3 * l)
        h = _norm_mod(xs, norm_mix, mod, l, 1, 0)
        proj = _in_proj(h, w_in_t, l)
        la = proj[:, LA_COL:]

        dkw = dict(layer=l, diff=True, q_gain=dq_gain[l:l + 1], k_gain=dk_gain[l:l + 1],
                   lam_vecs=diff_lambda[l], subln=diff_subln[l:l + 1], lam_init=lam_init)
        od_c, *diff_kv = _attention(proj, latent=False, prev_kv=diff_kv, **dkw)
        od_l = _attention(proj, latent=True, rope_tabs=rope_d, cache_k=cache_diff_k,
                          cache_v=cache_diff_v, **dkw)[0]
        gkw = dict(layer=l, diff=False, q_gain=gqa_q_norm[l:l + 1], k_gain=gqa_k_norm[l:l + 1])
        og_c, *gqa_kv = _attention(proj, latent=False, prev_kv=gqa_kv, **gkw)
        og_l = _attention(proj, latent=True, rope_tabs=rope_g, cache_k=cache_gqa_k,
                          cache_v=cache_gqa_v, **gkw)[0]
        ol_c, *gla_st = _gla(proj, la, wa_pad, gla_ba, gla_norm, masks, layer=l, latent=False,
                             prev_states=gla_st)
        ol_l = _gla(proj, la, wa_pad, gla_ba, gla_norm, masks, layer=l, latent=True,
                    s0_f=state_gla_fwd, s0_b=state_gla_bwd)[0]

        branches = ((od_c, og_c, ol_c), (od_l, og_l, ol_l))
        routed = []
        for latent in (False, True):
            m = _merge(h, branches[latent], w_gate, b_gate, w_branch, layer=l, latent=latent)
            x = _out_proj(m, w_out, xs[latent], mod, layer=l, latent=latent)
            hb, rank, aff = _route(x, norm_ffn, mod, w_router_t, layer=l, latent=latent)
            routed.append((x, *_gather(hb, rank, aff, latent=latent)))
        (x_c, xg_c, st_c, ws_c), (x_l, xg_l, st_l, ws_l) = routed
        ye = _experts(xg_c, xg_l, ws_c, ws_l, w_e_gate, w_e_up, w_e_down, l)
        xs = (_combine(st_c, ye, x_c, mod, layer=l, latent=False),
              _combine(st_l, ye, x_l, mod, layer=l, latent=True))

    caches = tuple(jnp.swapaxes(a, 0, 1) for a in (*diff_kv, *gqa_kv, *gla_st))
    return (xs[0].reshape(BATCH, SEQ, D_MODEL), xs[1].reshape(DEC_BATCH, DEC_SEQ, D_MODEL)) + caches
```

```python
import functools
import math
from typing import Callable, NamedTuple

import jax
import jax.numpy as jnp
from jax import lax
from jax.experimental import pallas as pl
from jax.experimental.pallas import tpu as pltpu

F32 = jnp.float32
BF16 = jnp.bfloat16

D_MODEL = 2048
BATCH = 16
SEQ = 256
DEPTH = 2
DEC_BATCH = 4
DEC_SEQ = 1024
PAST_LEN = 512
GRID_W = 64
BRANCH_WIDTH = 1024
HEAD = 128
DIFF_HEADS = 8
GQA_HEADS = 8
GQA_KV_HEADS = 2
GQA_GROUP = GQA_HEADS // GQA_KV_HEADS
GLA_HEADS = 4
GLA_DK = 128
GLA_DV = 256
GLA_GATE_RANK = 16
GLA_TAU = 16.0
GLA_BLOCK = 256
N_EXPERTS = 16
N_MOD = 6
ROPE_THETA = 10000.0
EPS = 1e-6

N_CTX = BATCH * SEQ
N_LAT = DEC_BATCH * DEC_SEQ
N_TOK = N_CTX + N_LAT
W_IN = 7712
DIFF_COL = 0
GQA_COL = 3 * DIFF_HEADS * HEAD
LA_COL = 7680
COND_ROWS = 8

VMEM_LIMIT = 56 * 1024 * 1024


def _cparams(n_axes):
    return pltpu.CompilerParams(dimension_semantics=("arbitrary",) * n_axes,
                                vmem_limit_bytes=VMEM_LIMIT)


def _dot(a, b):
    return jnp.dot(a, b, preferred_element_type=F32)


def _dot_nt(a, b):
    return lax.dot_general(a, b, (((1,), (1,)), ((), ())), preferred_element_type=F32)


def _rms(x):
    return x * lax.rsqrt(jnp.mean(x * x, axis=-1, keepdims=True) + EPS)


def _cond_row(start):
    return jnp.where(start < N_CTX, 0, 1 + (start - N_CTX) // DEC_SEQ)


def _mod_spec(chunk, tm):
    return pl.BlockSpec((None, None, 1, D_MODEL),
                        lambda i, *_: (0, _cond_row(i * tm), 0, chunk))


def _stream_specs(block, n_ctx_blocks, n_col_blocks=1):
    last_j = n_col_blocks - 1

    def ctx_map(i, *rest):
        j = rest[0] if n_col_blocks > 1 else 0
        own = i < n_ctx_blocks
        return (jnp.minimum(i, n_ctx_blocks - 1), jnp.where(own, j, last_j))

    def lat_map(i, *rest):
        j = rest[0] if n_col_blocks > 1 else 0
        own = i >= n_ctx_blocks
        return (jnp.maximum(i - n_ctx_blocks, 0), jnp.where(own, j, 0))

    return [pl.BlockSpec(block, ctx_map), pl.BlockSpec(block, lat_map)]


def _pick_stream(xc_ref, xl_ref, n_ctx_blocks):
    own_ctx = pl.program_id(0) < n_ctx_blocks
    return jnp.where(own_ctx, xc_ref[...], xl_ref[...])


def _mod_body(c_ref, w_ref, b_ref, o_ref):
    c = c_ref[...]
    a = (c * jax.nn.sigmoid(c)).astype(BF16)
    o_ref[...] = _dot(a, w_ref[...].astype(BF16)) + b_ref[...]


def _modulation(cond, w_mod, b_mod, *, layer, n_steps):
    n = N_MOD * D_MODEL
    tn = n // n_steps
    return _Plan(
        "ada_mod", _mod_body, (n_steps,),
        [pl.BlockSpec((COND_ROWS, D_MODEL), lambda j: (0, 0)),
         pl.BlockSpec((None, D_MODEL, tn), lambda j: (layer, 0, j)),
         pl.BlockSpec((None, 1, tn), lambda j: (layer, 0, j))],
        [cond, w_mod, b_mod.reshape(DEPTH, 1, n)],
        [pl.BlockSpec((COND_ROWS, tn), lambda j: (0, j))],
        [jax.ShapeDtypeStruct((COND_ROWS, n), F32)])


def _norm_mod_body(*refs, n_ctx_blocks):
    *x_refs, g_ref, sc_ref, sh_ref, o_ref = refs
    x = x_refs[0][...] if len(x_refs) == 1 else _pick_stream(*x_refs, n_ctx_blocks)
    y = _rms(x) * g_ref[...]
    o_ref[...] = (y * (1.0 + sc_ref[...]) + sh_ref[...]).astype(o_ref.dtype)


def _norm_mod(xs, gain, mod, layer, scale_chunk, shift_chunk):
    tm = 1024
    if len(xs) == 1:
        x_specs = [pl.BlockSpec((tm, D_MODEL), lambda i: (i, 0))]
    else:
        x_specs = _stream_specs((tm, D_MODEL), N_CTX // tm)
    return pl.pallas_call(
        functools.partial(_norm_mod_body, n_ctx_blocks=N_CTX // tm),
        grid=(N_TOK // tm,),
        in_specs=[*x_specs,
                  pl.BlockSpec((None, 1, D_MODEL), lambda i: (layer, 0, 0)),
                  _mod_spec(scale_chunk, tm),
                  _mod_spec(shift_chunk, tm)],
        out_specs=pl.BlockSpec((tm, D_MODEL), lambda i: (i, 0)),
        out_shape=jax.ShapeDtypeStruct((N_TOK, D_MODEL), BF16),
        compiler_params=_cparams(1),
        name="norm_mod",
    )(*xs, gain.reshape(DEPTH, 1, D_MODEL), mod, mod)


def _in_proj_body(h_ref, wt_ref, o_ref):
    o_ref[...] = _dot_nt(h_ref[...], wt_ref[...].astype(BF16))


def _in_proj(h, w_in_t, layer):
    tm, tn = 2048, 512
    return pl.pallas_call(
        _in_proj_body,
        grid=(N_TOK // tm, pl.cdiv(W_IN, tn)),
        in_specs=[pl.BlockSpec((tm, D_MODEL), lambda i, j: (i, 0)),
                  pl.BlockSpec((None, tn, D_MODEL), lambda i, j: (layer, j, 0))],
        out_specs=pl.BlockSpec((tm, tn), lambda i, j: (i, j)),
        out_shape=jax.ShapeDtypeStruct((N_TOK, W_IN), F32),
        compiler_params=_cparams(2),
        name="in_proj",
    )(h, w_in_t)


class _Plan(NamedTuple):
    name: str
    body: Callable
    grid: tuple
    in_specs: list
    args: list
    out_specs: list
    out_shape: list
    aliases: dict = {}


def _run(*plans):
    n_steps = math.prod(plans[0].grid)
    assert all(math.prod(p.grid) == n_steps for p in plans)

    def remap(spec, grid):
        def flat_map(i, index_map=spec.index_map):
            idx = []
            for size in reversed(grid):
                idx.append(i % size)
                i = i // size
            return index_map(*reversed(idx))
        return pl.BlockSpec(spec.block_shape, flat_map)

    n_in = [len(p.in_specs) for p in plans]
    n_out = [len(p.out_specs) for p in plans]

    def body(*refs):
        ins, outs = refs[:sum(n_in)], refs[sum(n_in):]
        for k, p in enumerate(plans):
            p.body(*ins[sum(n_in[:k]):sum(n_in[:k + 1])], *outs[sum(n_out[:k]):sum(n_out[:k + 1])])

    aliases = {sum(n_in[:k]) + i: sum(n_out[:k]) + o
               for k, p in enumerate(plans) for i, o in p.aliases.items()}
    res = pl.pallas_call(
        body, grid=(n_steps,),
        in_specs=[remap(s, p.grid) for p in plans for s in p.in_specs],
        out_specs=[remap(s, p.grid) for p in plans for s in p.out_specs],
        out_shape=[s for p in plans for s in p.out_shape], input_output_aliases=aliases,
        compiler_params=_cparams(1),
        name="+".join(p.name for p in plans),
    )(*[a for p in plans for a in p.args])
    return [res[sum(n_out[:k]):sum(n_out[:k + 1])] for k in range(len(plans))]


def _head_norm(x, g, split):
    x2 = x * x
    if split:
        lo = lax.broadcasted_iota(jnp.int32, x.shape, 1) < (HEAD // 2)
        s_lo = jnp.sum(jnp.where(lo, x2, 0.0), axis=-1, keepdims=True)
        s_hi = jnp.sum(jnp.where(lo, 0.0, x2), axis=-1, keepdims=True)
        ms = jnp.where(lo, s_lo, s_hi) * (2.0 / HEAD)
    else:
        ms = jnp.mean(x2, axis=-1, keepdims=True)
    return (x * lax.rsqrt(ms + EPS)) * g


def _rope(x, cos, sin_signed):
    even = (lax.broadcasted_iota(jnp.int32, x.shape, 1) % 2) == 0
    swapped = jnp.where(even, pltpu.roll(x, HEAD - 1, axis=1), pltpu.roll(x, 1, axis=1))
    return x * cos + swapped * sin_signed


def _with_ones(v):
    return jnp.concatenate([v.astype(BF16), jnp.ones(v.shape, BF16)], axis=1)


def _softmax_pv(qb, k_pieces, vext_pieces):
    scores = [_dot_nt(qb, kp) for kp in k_pieces]
    m = functools.reduce(jnp.maximum, [jnp.max(s, axis=-1, keepdims=True) for s in scores])
    acc = functools.reduce(lambda a, b: a + b,
                           [_dot(jnp.exp(s - m).astype(BF16), ve) for s, ve in zip(scores, vext_pieces)])
    return acc[:, :HEAD] / acc[:, HEAD:]


def _attn_body(*refs, nkv, nq, diff, latent, stacked, tq, n_sub, lam_init):
    it = iter(refs)
    q_ref, k_ref, v_ref, qg_ref, kg_ref = (next(it) for _ in range(5))
    if latent:
        cos_ref, sin_ref, ck_ref, cv_ref = (next(it) for _ in range(4))
    if diff:
        lam_ref, sub_ref = next(it), next(it)
    if stacked:
        kprev_ref, vprev_ref = next(it), next(it)
    o_ref = next(it)
    if not latent:
        kn_ref, vo_ref = next(it), next(it)

    if diff:
        lv = lam_ref[...]
        lam = (jnp.exp(jnp.sum(lv[0:1] * lv[1:2], axis=-1, keepdims=True))
               - jnp.exp(jnp.sum(lv[2:3] * lv[3:4], axis=-1, keepdims=True)) + lam_init)
        qk_scale = (HEAD // 2) ** -0.5
    else:
        qk_scale = HEAD ** -0.5
    qg = qg_ref[...]

    for kv in range(nkv):
        kcols = slice(kv * HEAD, (kv + 1) * HEAD)
        v = v_ref[:, kcols]
        kn = _head_norm(k_ref[:, kcols], kg_ref[...], diff)
        if not latent:
            if stacked:
                kn_ref[0, kv] = kprev_ref[kv]
                vo_ref[0, kv] = vprev_ref[kv]
                kn_ref[1, kv] = kn
                vo_ref[1, kv] = v
            else:
                kn_ref[kv] = kn
                vo_ref[kv] = v
        if latent:
            kn = _rope(kn, cos_ref[...], sin_ref[...])
        k_pieces = [kn.astype(BF16)]
        vext_pieces = [_with_ones(v)]
        if latent:
            k_pieces.append(ck_ref[kv].astype(BF16))
            vext_pieces.append(_with_ones(cv_ref[kv]))

        def one_tile(rows, j, k_pieces=k_pieces, vext_pieces=vext_pieces, kv=kv):
            cols = slice((kv * nq + j) * HEAD, (kv * nq + j + 1) * HEAD)
            qn = _head_norm(q_ref[rows, cols], qg, diff)
            if latent:
                qn = _rope(qn, cos_ref[rows, :], sin_ref[rows, :])
            qn = qn * qk_scale
            if diff:
                lo = lax.broadcasted_iota(jnp.int32, qn.shape, 1) < (HEAD // 2)
                o1 = _softmax_pv(jnp.where(lo, qn, 0.0).astype(BF16), k_pieces, vext_pieces)
                o2 = _softmax_pv(jnp.where(lo, 0.0, qn).astype(BF16), k_pieces, vext_pieces)
                o = (_rms(o1 - lam * o2) * sub_ref[...]) * (1.0 - lam_init)
            else:
                o = _softmax_pv(qn.astype(BF16), k_pieces, vext_pieces)
            o_ref[rows, cols] = o.astype(o_ref.dtype)

        for j in range(nq):
            for i in range(n_sub):
                one_tile(slice(i * tq, (i + 1) * tq), j)


def _attention(proj, *, layer, latent, diff, q_gain, k_gain, rope_tabs=None,
               cache_k=None, cache_v=None, lam_vecs=None, subln=None, lam_init=0.0, prev_kv=None):
    t = DEC_SEQ if latent else SEQ
    nb = DEC_BATCH if latent else BATCH
    row0 = N_CTX // t if latent else 0
    nq, nkv_all, col0 = (1, DIFF_HEADS, DIFF_COL) if diff else (GQA_GROUP, GQA_KV_HEADS, GQA_COL)
    nkv = 1 if latent else nkv_all
    wq, wk = nkv * nq * HEAD, nkv * HEAD
    qblk = col0 // wq
    kblk = (col0 + nkv_all * nq * HEAD) // wk
    vblk = (col0 + nkv_all * (nq + 1) * HEAD) // wk
    tq = 256
    n_sub = t // tq
    grid = (nb, nkv_all) if latent else (nb,)
    hix = (lambda g: g[1]) if latent else (lambda g: 0)

    def const(*shape):
        return pl.BlockSpec(shape, lambda *g: (0,) * len(shape))

    in_specs = [pl.BlockSpec((t, wq), lambda *g: (row0 + g[0], qblk + hix(g))),
                pl.BlockSpec((t, wk), lambda *g: (row0 + g[0], kblk + hix(g))),
                pl.BlockSpec((t, wk), lambda *g: (row0 + g[0], vblk + hix(g))),
                const(1, HEAD), const(1, HEAD)]
    args = [proj, proj, proj, q_gain, k_gain]
    if latent:
        cspec = pl.BlockSpec((None, None, 1, PAST_LEN, HEAD), lambda *g: (g[0], layer, g[1], 0, 0))
        in_specs += [const(t, HEAD), const(t, HEAD), cspec, cspec]
        args += [*rope_tabs, cache_k, cache_v]
    if diff:
        in_specs += [const(4, HEAD // 2), const(1, HEAD)]
        args += [lam_vecs, subln]
    stacked = prev_kv is not None
    if stacked:
        pspec = pl.BlockSpec((None, nkv, t, HEAD), lambda *g: (g[0], 0, 0, 0))
        in_specs += [pspec, pspec]
        args += list(prev_kv)

    out_specs = [pl.BlockSpec((t, wq), lambda *g: (g[0], hix(g)))]
    out_shape = [jax.ShapeDtypeStruct((nb * t, BRANCH_WIDTH), BF16)]
    if not latent:
        if stacked:
            kv_spec = pl.BlockSpec((DEPTH, None, nkv, t, HEAD), lambda *g: (0, g[0], 0, 0, 0))
            kv_shape = jax.ShapeDtypeStruct((DEPTH, nb, nkv, t, HEAD), F32)
        else:
            kv_spec = pl.BlockSpec((None, nkv, t, HEAD), lambda *g: (g[0], 0, 0, 0))
            kv_shape = jax.ShapeDtypeStruct((nb, nkv, t, HEAD), F32)
        out_specs += [kv_spec, kv_spec]
        out_shape += [kv_shape, kv_shape]

    body = functools.partial(_attn_body, nkv=nkv, nq=nq, diff=diff, latent=latent, stacked=stacked,
                             tq=tq, n_sub=n_sub, lam_init=lam_init)
    name = ("diff" if diff else "gqa") + ("_lat" if latent else "_ctx")
    return _Plan(name, body, grid, in_specs, args, out_specs, out_shape)


def _log_sigmoid(x):
    return jnp.minimum(x, 0.0) - jnp.log1p(jnp.exp(-jnp.abs(x)))


def _seg_bcast(x, width, pos, needed):
    n = x.shape[0]
    if width >= 8:
        g = x.reshape(n // width, width, x.shape[1])[:, pos:pos + 1, :]
        return jnp.broadcast_to(g, (n // width, width, x.shape[1])).reshape(x.shape)
    off = lax.broadcasted_iota(jnp.int32, (n, 1), 0) % width
    y = None
    for j in needed:
        cand = x if j == pos else pltpu.roll(x, (j - pos) % n, axis=0)
        y = cand if y is None else jnp.where(off == j, cand, y)
    return y


def _gla_block(q, k, a_f, a_b, mask_ref):
    n = q.shape[0]
    row = lax.broadcasted_iota(jnp.int32, (n, 1), 0)
    pf, qf, pb, qb = a_f, a_f, a_b, a_b
    acc = _dot_nt(q.astype(BF16), k.astype(BF16)) * mask_ref[0]
    half, level = 1, 1
    while half < n:
        width = 2 * half
        right = (row % width) >= half
        eq = jnp.where(right, pf, qb)
        ek = jnp.where(right, pb - a_b, qf - a_f)
        s = _dot_nt((q * jnp.exp(eq)).astype(BF16), (k * jnp.exp(ek)).astype(BF16))
        acc = acc + s * mask_ref[level]
        lo_pos, hi_pos = list(range(half)), list(range(half, width))
        pf = pf + jnp.where(right, _seg_bcast(pf, width, half - 1, hi_pos), 0.0)
        pb = pb + jnp.where(right, _seg_bcast(pb, width, half - 1, hi_pos), 0.0)
        qf = qf + jnp.where(right, 0.0, _seg_bcast(qf, width, half, lo_pos))
        qb = qb + jnp.where(right, 0.0, _seg_bcast(qb, width, half, lo_pos))
        half, level = width, level + 1
    return acc, pf, qf, pb, qb


def _gla_body(*refs, n_blocks, n_heads, latent, stacked):
    it = iter(refs)
    q_ref, k_ref, v_ref, r_ref, z_ref, wa_ref, ba_ref, gn_ref, mask_ref = (next(it) for _ in range(9))
    if latent:
        s0f_ref, s0b_ref = next(it), next(it)
    if stacked:
        sfprev_ref, sbprev_ref = next(it), next(it)
    o_ref = next(it)
    if not latent:
        sf_ref, sb_ref = next(it), next(it)
    n = GLA_BLOCK
    wa = wa_ref[...].astype(BF16)
    ba = ba_ref[...]

    for hh in range(n_heads):
        kcols = slice(hh * GLA_DK, (hh + 1) * GLA_DK)
        vcols = slice(hh * GLA_DV, (hh + 1) * GLA_DV)
        blocks = []
        for c in range(n_blocks):
            rows = slice(c * n, (c + 1) * n)
            z = z_ref[rows, :].astype(BF16)
            a_f = _log_sigmoid(_dot(z, wa[0][:, kcols]) + ba[0:1, kcols]) * (1.0 / GLA_TAU)
            a_b = _log_sigmoid(_dot(z, wa[1][:, kcols]) + ba[1:2, kcols]) * (1.0 / GLA_TAU)
            q = q_ref[rows, kcols] * (GLA_DK ** -0.5)
            k = k_ref[rows, kcols]
            vb = v_ref[rows, vcols].astype(BF16)
            acc, pf, qf, pb, qb = _gla_block(q, k, a_f, a_b, mask_ref)
            o = _dot(acc.astype(BF16), vb)
            kf = (k * jnp.exp(qf - a_f)).T.astype(BF16)
            kb = (k * jnp.exp(pb - a_b)).T.astype(BF16)
            blocks.append(dict(o=o, u_f=_dot(kf, vb), u_b=_dot(kb, vb),
                               q_f=(q * jnp.exp(pf)).astype(BF16), q_b=(q * jnp.exp(qb)).astype(BF16),
                               tot_f=pf.T[:, n - 1:n], tot_b=qb.T[:, 0:1]))

        if latent:
            s = s0f_ref[hh]
            for c in range(n_blocks):
                blk = blocks[c]
                blk["o"] = blk["o"] + _dot(blk["q_f"], s.astype(BF16))
                s = jnp.exp(blk["tot_f"]) * s + blk["u_f"]
            s = s0b_ref[hh]
            for c in reversed(range(n_blocks)):
                blk = blocks[c]
                blk["o"] = blk["o"] + _dot(blk["q_b"], s.astype(BF16))
                s = jnp.exp(blk["tot_b"]) * s + blk["u_b"]
        elif stacked:
            sf_ref[0, hh] = sfprev_ref[hh]
            sb_ref[0, hh] = sbprev_ref[hh]
            sf_ref[1, hh] = blocks[0]["u_f"]
            sb_ref[1, hh] = blocks[0]["u_b"]
        else:
            sf_ref[hh] = blocks[0]["u_f"]
            sb_ref[hh] = blocks[0]["u_b"]

        for c in range(n_blocks):
            rows = slice(c * n, (c + 1) * n)
            r = r_ref[rows, vcols]
            out = (_rms(blocks[c]["o"]) * gn_ref[...]) * (r * jax.nn.sigmoid(r))
            o_ref[rows, vcols] = out.astype(o_ref.dtype)


def _gla_masks():
    n = GLA_BLOCK
    t = jnp.arange(n)[:, None]
    s = jnp.arange(n)[None, :]
    masks = [2.0 * (t == s)]
    half = 1
    while half < n:
        width = 2 * half
        masks.append((t // width == s // width) & ((t % width >= half) != (s % width >= half)))
        half = width
    return jnp.stack([m.astype(F32) for m in masks])


def _gla(proj, la, wa_pad, ba, gnorm, masks, *, layer, latent, s0_f=None, s0_b=None,
         prev_states=None):
    t = DEC_SEQ if latent else SEQ
    nb = DEC_BATCH if latent else BATCH
    row0 = N_CTX // t if latent else 0
    nh = 2
    wk, wv = nh * GLA_DK, nh * GLA_DV
    qc, kc = 36 * GLA_DK // wk, 40 * GLA_DK // wk
    vc, rc = 22 * GLA_DV // wv, 26 * GLA_DV // wv
    n_lvl = masks.shape[0]
    in_specs = [pl.BlockSpec((t, wk), lambda b, h: (row0 + b, qc + h)),
                pl.BlockSpec((t, wk), lambda b, h: (row0 + b, kc + h)),
                pl.BlockSpec((t, wv), lambda b, h: (row0 + b, vc + h)),
                pl.BlockSpec((t, wv), lambda b, h: (row0 + b, rc + h)),
                pl.BlockSpec((t, 2 * GLA_GATE_RANK), lambda b, h: (row0 + b, 0)),
                pl.BlockSpec((None, 2, 2 * GLA_GATE_RANK, wk), lambda b, h: (layer, 0, 0, h)),
                pl.BlockSpec((None, 2, wk), lambda b, h: (layer, 0, h)),
                pl.BlockSpec((None, 1, GLA_DV), lambda b, h: (layer, 0, 0)),
                pl.BlockSpec((n_lvl, GLA_BLOCK, GLA_BLOCK), lambda b, h: (0, 0, 0))]
    args = [proj, proj, proj, proj, la, wa_pad, ba, gnorm.reshape(DEPTH, 1, GLA_DV), masks]
    if latent:
        sspec = pl.BlockSpec((None, None, nh, GLA_DK, GLA_DV), lambda b, h: (b, layer, h, 0, 0))
        in_specs += [sspec, sspec]
        args += [s0_f, s0_b]
    stacked = prev_states is not None
    if stacked:
        pspec = pl.BlockSpec((None, nh, GLA_DK, GLA_DV), lambda b, h: (b, h, 0, 0))
        in_specs += [pspec, pspec]
        args += list(prev_states)
    out_specs = [pl.BlockSpec((t, wv), lambda b, h: (b, h))]
    out_shape = [jax.ShapeDtypeStruct((nb * t, BRANCH_WIDTH), BF16)]
    if not latent:
        if stacked:
            st_spec = pl.BlockSpec((DEPTH, None, nh, GLA_DK, GLA_DV), lambda b, h: (0, b, h, 0, 0))
            st_shape = jax.ShapeDtypeStruct((DEPTH, nb, GLA_HEADS, GLA_DK, GLA_DV), F32)
        else:
            st_spec = pl.BlockSpec((None, nh, GLA_DK, GLA_DV), lambda b, h: (b, h, 0, 0))
            st_shape = jax.ShapeDtypeStruct((nb, GLA_HEADS, GLA_DK, GLA_DV), F32)
        out_specs += [st_spec, st_spec]
        out_shape += [st_shape, st_shape]
    body = functools.partial(_gla_body, n_blocks=t // GLA_BLOCK, n_heads=nh, latent=latent,
                             stacked=stacked)
    return _Plan("gla_lat" if latent else "gla_ctx", body, (nb, GLA_HEADS // nh), in_specs, args,
                 out_specs, out_shape)


def _merge_body(h_ref, od_ref, og_ref, ol_ref, wg0_ref, wg1_ref, wg2_ref, b0_ref, b1_ref, b2_ref,
                wb_ref, o_ref):
    h = h_ref[...]
    acc = None
    for n, (br_ref, wg_ref, b_ref) in enumerate(((od_ref, wg0_ref, b0_ref), (og_ref, wg1_ref, b1_ref),
                                                 (ol_ref, wg2_ref, b2_ref))):
        gate = jax.nn.sigmoid(_dot(h, wg_ref[...].astype(BF16)) + b_ref[...])
        term = gate * _dot(br_ref[...], wb_ref[n].astype(BF16))
        acc = term if acc is None else acc + term
    o_ref[...] = acc.astype(o_ref.dtype)


def _merge(h, branches, w_gate, b_gate, w_branch, *, layer, latent):
    tm, tn = 1024, 256
    nj = D_MODEL // tn
    n_rows = N_LAT if latent else N_CTX
    row0 = N_CTX // tm if latent else 0
    b3 = b_gate.reshape(DEPTH, 1, 3 * D_MODEL)
    br_spec = pl.BlockSpec((tm, BRANCH_WIDTH), lambda i, j: (i, 0))
    wg_specs = [pl.BlockSpec((None, D_MODEL, tn), lambda i, j, n=n: (layer, 0, n * nj + j)) for n in range(3)]
    b_specs = [pl.BlockSpec((None, 1, tn), lambda i, j, n=n: (layer, 0, n * nj + j)) for n in range(3)]
    return pl.pallas_call(
        _merge_body,
        grid=(n_rows // tm, nj),
        in_specs=[pl.BlockSpec((tm, D_MODEL), lambda i, j: (row0 + i, 0)), br_spec, br_spec, br_spec,
                  *wg_specs, *b_specs,
                  pl.BlockSpec((None, 3, BRANCH_WIDTH, tn), lambda i, j: (layer, 0, 0, j))],
        out_specs=pl.BlockSpec((tm, tn), lambda i, j: (i, j)),
        out_shape=jax.ShapeDtypeStruct((n_rows, D_MODEL), BF16),
        compiler_params=_cparams(2),
        name="merge_lat" if latent else "merge_ctx",
    )(h, *branches, w_gate, w_gate, w_gate, b3, b3, b3, w_branch)


def _out_proj_body(m_ref, w_ref, x_ref, *rest):
    *g_refs, o_ref = rest
    y = _dot(m_ref[...], w_ref[...].astype(BF16))
    rows = y.shape[0] // len(g_refs)
    for k, g_ref in enumerate(g_refs):
        sl = slice(k * rows, (k + 1) * rows)
        o_ref[sl, :] = x_ref[sl, :] + g_ref[...] * y[sl, :]


def _out_proj(m, w_out, x, mod, *, layer, latent):
    tm, tn = 2048, 512
    nj = D_MODEL // tn
    n_rows = N_LAT if latent else N_CTX
    row0 = N_CTX if latent else 0
    n_gates = tm // DEC_SEQ
    g_specs = [pl.BlockSpec((None, None, 1, tn),
                            lambda i, j, k=k: (0, _cond_row(row0 + i * tm + k * DEC_SEQ), 0, 2 * nj + j))
               for k in range(n_gates)]
    return pl.pallas_call(
        _out_proj_body,
        grid=(n_rows // tm, nj),
        in_specs=[pl.BlockSpec((tm, D_MODEL), lambda i, j: (i, 0)),
                  pl.BlockSpec((None, D_MODEL, tn), lambda i, j: (layer, 0, j)),
                  pl.BlockSpec((tm, tn), lambda i, j: (i, j)),
                  *g_specs],
        out_specs=pl.BlockSpec((tm, tn), lambda i, j: (i, j)),
        out_shape=jax.ShapeDtypeStruct((n_rows, D_MODEL), F32),
        compiler_params=_cparams(2),
        name="out_proj_lat" if latent else "out_proj_ctx",
    )(m, w_out, x, *([mod] * n_gates))


def _route_body(x_ref, g_ref, sc_ref, sh_ref, wr_ref, xg_ref, st_ref, ws_ref,
                hb_ref, afft_ref, aff_ref, *, t, cap, group):
    g = pl.program_id(1)

    @pl.when(g == 0)
    def _():
        y = _rms(x_ref[...]) * g_ref[...]
        hb = (y * (1.0 + sc_ref[...]) + sh_ref[...]).astype(BF16)
        hb_ref[...] = hb
        logits = _dot_nt(wr_ref[...].astype(BF16), hb)
        e = jnp.exp(logits - jnp.max(logits, axis=0, keepdims=True))
        aff_t = e / jnp.sum(e, axis=0, keepdims=True)
        afft_ref[...] = aff_t
        aff_ref[...] = aff_t.T

    blk = HEAD
    nblk = t // blk
    earlier = (lax.broadcasted_iota(jnp.int32, (blk, blk), 0) < lax.broadcasted_iota(jnp.int32, (blk, blk), 1))
    slot = lax.broadcasted_iota(jnp.int32, (cap, t), 0).astype(F32)

    def count(mask):
        return jnp.sum(mask.astype(F32), axis=0, keepdims=True)

    def rank_of(ex):
        col = jnp.broadcast_to(aff_ref[:, ex:ex + 1], (t, blk))
        parts = []
        for j in range(nblk):
            row = afft_ref[ex:ex + 1, j * blk:(j + 1) * blk]
            cd = col[j * blk:(j + 1) * blk]
            cnt = count((cd > row) | ((cd == row) & earlier))
            if j > 0:
                cnt = cnt + count(col[:j * blk] >= row)
            if j < nblk - 1:
                cnt = cnt + count(col[(j + 1) * blk:] > row)
            parts.append(cnt)
        return jnp.concatenate(parts, axis=1)

    for k in range(N_EXPERTS // group):
        @pl.when(g == k)
        def _(k=k):
            sels = []
            for i in range(group):
                ex = k * group + i
                sel = (rank_of(ex) == slot).astype(F32)
                ws_ref[i] = jnp.sum(sel * afft_ref[ex:ex + 1, :], axis=1, keepdims=True)
                sels.append(sel)
            sel_g = jnp.concatenate(sels, axis=0)
            xg = _dot(sel_g.astype(BF16), hb_ref[...]).astype(BF16)
            xg_ref[...] = xg.reshape(group, cap, D_MODEL)
            st_ref[...] = sel_g.T.astype(BF16)


def _route(x, gain, mod, w_router_t, *, layer, latent):
    t = DEC_SEQ if latent else SEQ
    nb = DEC_BATCH if latent else BATCH
    row0 = N_CTX // t if latent else 0
    cap = 2 * t // N_EXPERTS
    group = 512 // cap
    return pl.pallas_call(
        functools.partial(_route_body, t=t, cap=cap, group=group),
        grid=(nb, N_EXPERTS // group),
        in_specs=[pl.BlockSpec((t, D_MODEL), lambda b, g: (b, 0)),
                  pl.BlockSpec((None, 1, D_MODEL), lambda b, g: (layer, 0, 0)),
                  pl.BlockSpec((None, None, 1, D_MODEL), lambda b, g: (0, _cond_row((row0 + b) * t), 0, 4)),
                  pl.BlockSpec((None, None, 1, D_MODEL), lambda b, g: (0, _cond_row((row0 + b) * t), 0, 3)),
                  pl.BlockSpec((None, N_EXPERTS, D_MODEL), lambda b, g: (layer, 0, 0))],
        out_specs=[pl.BlockSpec((group, cap, D_MODEL), lambda b, g: (g, b, 0)),
                   pl.BlockSpec((t, group * cap), lambda b, g: (b, g)),
                   pl.BlockSpec((group, cap, 1), lambda b, g: (g, b, 0))],
        out_shape=[jax.ShapeDtypeStruct((N_EXPERTS, nb * cap, D_MODEL), BF16),
                   jax.ShapeDtypeStruct((nb * t, N_EXPERTS * cap), BF16),
                   jax.ShapeDtypeStruct((N_EXPERTS, nb * cap, 1), F32)],
        scratch_shapes=[pltpu.VMEM((t, D_MODEL), BF16), pltpu.VMEM((N_EXPERTS, t), F32),
                        pltpu.VMEM((t, N_EXPERTS), F32)],
        compiler_params=_cparams(2),
        name="route_lat" if latent else "route_ctx",
    )(x, gain.reshape(DEPTH, 1, D_MODEL), mod, mod, w_router_t)


def _experts_body(xc_ref, xl_ref, wsc_ref, wsl_ref, wg_ref, wu_ref, wd_ref, o_ref, acc_ref, *, nf, half):
    f = pl.program_id(1)

    def step(first, last):
        wg = wg_ref[...].astype(BF16)
        wu = wu_ref[...].astype(BF16)
        wd = wd_ref[...].astype(BF16)
        for s, (x_ref, ws_ref) in enumerate(((xc_ref, wsc_ref), (xl_ref, wsl_ref))):
            rows = slice(s * half, (s + 1) * half)
            x = x_ref[...]
            g = _dot(x, wg)
            hid = (g * jax.nn.sigmoid(g)) * _dot(x, wu)
            y = _dot(hid.astype(BF16), wd)
            if not first:
                y = acc_ref[rows, :] + y
            if last:
                o_ref[rows, :] = (y * ws_ref[...]).astype(o_ref.dtype)
            else:
                acc_ref[rows, :] = y

    pl.when(f == 0)(lambda: step(True, False))
    pl.when((f > 0) & (f < nf - 1))(lambda: step(False, False))
    pl.when(f == nf - 1)(lambda: step(False, True))


def _experts(xg_ctx, xg_lat, ws_ctx, ws_lat, w_gate, w_up, w_down, layer):
    half = xg_ctx.shape[1]
    tf = 512
    nf = D_MODEL // tf
    x_spec = pl.BlockSpec((None, half, D_MODEL), lambda e, f: (e, 0, 0))
    ws_spec = pl.BlockSpec((None, half, 1), lambda e, f: (e, 0, 0))
    return pl.pallas_call(
        functools.partial(_experts_body, nf=nf, half=half),
        grid=(N_EXPERTS, nf),
        in_specs=[x_spec, x_spec, ws_spec, ws_spec,
                  pl.BlockSpec((None, None, D_MODEL, tf), lambda e, f: (layer, e, 0, f)),
                  pl.BlockSpec((None, None, D_MODEL, tf), lambda e, f: (layer, e, 0, f)),
                  pl.BlockSpec((None, None, tf, D_MODEL), lambda e, f: (layer, e, f, 0))],
        out_specs=pl.BlockSpec((None, 2 * half, D_MODEL), lambda e, f: (e, 0, 0)),
        out_shape=jax.ShapeDtypeStruct((N_EXPERTS, 2 * half, D_MODEL), BF16),
        scratch_shapes=[pltpu.VMEM((2 * half, D_MODEL), F32)],
        compiler_params=_cparams(2),
        name="experts",
    )(xg_ctx, xg_lat, ws_ctx, ws_lat, w_gate, w_up, w_down)


def _combine_body(st_ref, ye_ref, x_ref, g_ref, o_ref, *, cap):
    ye = ye_ref[...].reshape(N_EXPERTS * cap, ye_ref.shape[-1])
    o_ref[...] = x_ref[...] + g_ref[...] * _dot(st_ref[...], ye)


def _combine(st, ye, x, mod, *, layer, latent):
    t = DEC_SEQ if latent else SEQ
    nb = DEC_BATCH if latent else BATCH
    cap = 2 * t // N_EXPERTS
    row0 = N_CTX // t if latent else 0
    slot0 = (BATCH * 2 * SEQ // N_EXPERTS) // cap if latent else 0
    tn = 512 if latent else D_MODEL
    nj = D_MODEL // tn
    return _Plan(
        "combine_lat" if latent else "combine_ctx", functools.partial(_combine_body, cap=cap), (nb, nj),
        [pl.BlockSpec((t, N_EXPERTS * cap), lambda b, j: (b, 0)),
         pl.BlockSpec((N_EXPERTS, cap, tn), lambda b, j: (0, slot0 + b, j)),
         pl.BlockSpec((t, tn), lambda b, j: (b, j)),
         pl.BlockSpec((None, None, 1, tn), lambda b, j: (0, _cond_row((row0 + b) * t), 0, 5 * nj + j))],
        [st, ye, x, mod],
        [pl.BlockSpec((t, tn), lambda b, j: (b, j))],
        [jax.ShapeDtypeStruct((nb * t, D_MODEL), F32)],
        {2: 0})


def _rope_tables(t, dim):
    rows = t // GRID_W
    row = jnp.repeat(jnp.arange(rows), GRID_W)
    col = jnp.tile(jnp.arange(GRID_W), rows)

    def angles(pos, d):
        inv = ROPE_THETA ** (-jnp.arange(0, d, 2, dtype=F32) / d)
        return pos.astype(F32)[:, None] * inv[None, :]

    ang = jnp.concatenate([angles(row, dim // 2), angles(col, dim // 2)], axis=-1)
    cos = jnp.repeat(jnp.cos(ang), 2, axis=-1)
    sin = jnp.repeat(jnp.sin(ang), 2, axis=-1) * jnp.tile(jnp.array([-1.0, 1.0], F32), dim // 2)
    reps = HEAD // dim
    return jnp.tile(cos, (1, reps)), jnp.tile(sin, (1, reps))


def kernel(x_prompt, x_sample, cache_diff_k, cache_diff_v, cache_gqa_k, cache_gqa_v, state_gla_fwd,
           state_gla_bwd, c, c_ctx, w_mod, b_mod, norm_mix, w_in, diff_q_norm, diff_k_norm,
           diff_lambda, diff_subln, gqa_q_norm, gqa_k_norm, gla_wa2, gla_ba, gla_norm, w_branch,
           w_gate, b_gate, w_out, norm_ffn, w_router, w_e_gate, w_e_up, w_e_down):
    xs = (x_prompt.reshape(N_CTX, D_MODEL), x_sample.reshape(N_LAT, D_MODEL))
    cond = jnp.concatenate([c_ctx[None, :], c, jnp.zeros((COND_ROWS - 1 - DEC_BATCH, D_MODEL), F32)], axis=0)
    mod_shape = (1, COND_ROWS, 1, N_MOD * D_MODEL)
    mod = _run(_modulation(cond, w_mod, b_mod, layer=0, n_steps=12))[0][0].reshape(mod_shape)

    rope_d = _rope_tables(DEC_SEQ, HEAD // 2)
    rope_g = _rope_tables(DEC_SEQ, HEAD)
    masks = _gla_masks()
    zeros = jnp.zeros((DEPTH, GLA_GATE_RANK, GLA_HEADS * GLA_DK), F32)
    wa_pad = jnp.stack([jnp.concatenate([gla_wa2[:, 0], zeros], axis=1),
                        jnp.concatenate([zeros, gla_wa2[:, 1]], axis=1)], axis=1)
    dq_gain = jnp.tile(diff_q_norm, (1, 2))
    dk_gain = jnp.tile(diff_k_norm, (1, 2))
    w_router_t = jnp.swapaxes(w_router, 1, 2)
    w_in_t = jnp.swapaxes(w_in, 1, 2)

    diff_kv = gqa_kv = gla_st = None
    for l in range(DEPTH):
        lam_init = 0.8 - 0.6 * math.exp(-0.3 * l)
        h = _norm_mod(xs, norm_mix, mod, l, 1, 0)
        proj = _in_proj(h, w_in_t, l)
        la = proj[:, LA_COL:]

        dkw = dict(layer=l, diff=True, q_gain=dq_gain[l:l + 1], k_gain=dk_gain[l:l + 1],
                   lam_vecs=diff_lambda[l], subln=diff_subln[l:l + 1], lam_init=lam_init)
        gkw = dict(layer=l, diff=False, q_gain=gqa_q_norm[l:l + 1], k_gain=gqa_k_norm[l:l + 1])
        (od_c, *diff_kv), (og_c, *gqa_kv) = _run(
            _attention(proj, latent=False, prev_kv=diff_kv, **dkw),
            _attention(proj, latent=False, prev_kv=gqa_kv, **gkw))
        plans = [
            _gla(proj, la, wa_pad, gla_ba, gla_norm, masks, layer=l, latent=False, prev_states=gla_st),
            _attention(proj, latent=True, rope_tabs=rope_d, cache_k=cache_diff_k, cache_v=cache_diff_v, **dkw)]
        if l + 1 < DEPTH:
            plans.append(_modulation(cond, w_mod, b_mod, layer=l + 1, n_steps=math.prod(plans[0].grid)))
        (ol_c, *gla_st), (od_l,), *next_mod = _run(*plans)
        (ol_l,), (og_l,) = _run(
            _gla(proj, la, wa_pad, gla_ba, gla_norm, masks, layer=l, latent=True,
                 s0_f=state_gla_fwd, s0_b=state_gla_bwd),
            _attention(proj, latent=True, rope_tabs=rope_g, cache_k=cache_gqa_k, cache_v=cache_gqa_v, **gkw))

        branches = ((od_c, og_c, ol_c), (od_l, og_l, ol_l))
        routed = []
        for latent in (False, True):
            m = _merge(h, branches[latent], w_gate, b_gate, w_branch, layer=l, latent=latent)
            x = _out_proj(m, w_out, xs[latent], mod, layer=l, latent=latent)
            routed.append((x, *_route(x, norm_ffn, mod, w_router_t, layer=l, latent=latent)))
        (x_c, xg_c, st_c, ws_c), (x_l, xg_l, st_l, ws_l) = routed
        ye = _experts(xg_c, xg_l, ws_c, ws_l, w_e_gate, w_e_up, w_e_down, l)
        (x_c,), (x_l,) = _run(_combine(st_c, ye, x_c, mod, layer=l, latent=False),
                              _combine(st_l, ye, x_l, mod, layer=l, latent=True))
        xs = (x_c, x_l)
        if next_mod:
            mod = next_mod[0][0].reshape(mod_shape)

    caches = tuple(jnp.swapaxes(a, 0, 1) for a in (*diff_kv, *gqa_kv, *gla_st))
    return (xs[0].reshape(BATCH, SEQ, D_MODEL), xs[1].reshape(DEC_BATCH, DEC_SEQ, D_MODEL)) + caches
```

```python
import functools
import math
from typing import Callable, NamedTuple

import jax
import jax.numpy as jnp
from jax import lax
from jax.experimental import pallas as pl
from jax.experimental.pallas import tpu as pltpu

F32 = jnp.float32
BF16 = jnp.bfloat16

D_MODEL = 2048
BATCH = 16
SEQ = 256
DEPTH = 2
DEC_BATCH = 4
DEC_SEQ = 1024
PAST_LEN = 512
GRID_W = 64
BRANCH_WIDTH = 1024
HEAD = 128
DIFF_HEADS = 8
GQA_HEADS = 8
GQA_KV_HEADS = 2
GQA_GROUP = GQA_HEADS // GQA_KV_HEADS
GLA_HEADS = 4
GLA_DK = 128
GLA_DV = 256
GLA_GATE_RANK = 16
GLA_TAU = 16.0
GLA_BLOCK = 256
N_EXPERTS = 16
N_MOD = 6
ROPE_THETA = 10000.0
EPS = 1e-6

N_CTX = BATCH * SEQ
N_LAT = DEC_BATCH * DEC_SEQ
N_TOK = N_CTX + N_LAT
W_IN = 7712
DIFF_COL = 0
GQA_COL = 3 * DIFF_HEADS * HEAD
LA_COL = 7680
COND_ROWS = 8

VMEM_LIMIT = 56 * 1024 * 1024


def _cparams(n_axes):
    return pltpu.CompilerParams(dimension_semantics=("arbitrary",) * n_axes,
                                vmem_limit_bytes=VMEM_LIMIT)


def _dot(a, b):
    return jnp.dot(a, b, preferred_element_type=F32)


def _dot_nt(a, b):
    return lax.dot_general(a, b, (((1,), (1,)), ((), ())), preferred_element_type=F32)


def _rms(x):
    return x * lax.rsqrt(jnp.mean(x * x, axis=-1, keepdims=True) + EPS)


def _cond_row(start):
    return jnp.where(start < N_CTX, 0, 1 + (start - N_CTX) // DEC_SEQ)


class _Mod(NamedTuple):
    table: jax.Array
    chunk0: int


def _mod_spec(mod, chunk, tm):
    return pl.BlockSpec((None, None, 1, D_MODEL),
                        lambda i, *_: (0, _cond_row(i * tm), 0, chunk - mod.chunk0))


def _stream_specs(block, n_ctx_blocks):
    return [pl.BlockSpec(block, lambda i: (jnp.minimum(i, n_ctx_blocks - 1), 0)),
            pl.BlockSpec(block, lambda i: (jnp.maximum(i - n_ctx_blocks, 0), 0))]


def _mod_body(c_ref, w_ref, b_ref, o_ref):
    c = c_ref[...]
    a = (c * jax.nn.sigmoid(c)).astype(BF16)
    o_ref[...] = _dot(a, w_ref[...].astype(BF16)) + b_ref[...]


def _modulation(cond, w_mod, b_mod, *, layer, chunks, n_steps):
    n = (chunks[1] - chunks[0]) * D_MODEL
    tn = n // n_steps
    j0 = chunks[0] * D_MODEL // tn
    return _Plan(
        "ada_mod", _mod_body, (n_steps,),
        [pl.BlockSpec((COND_ROWS, D_MODEL), lambda j: (0, 0)),
         pl.BlockSpec((None, D_MODEL, tn), lambda j: (layer, 0, j0 + j)),
         pl.BlockSpec((None, 1, tn), lambda j: (layer, 0, j0 + j))],
        [cond, w_mod, b_mod.reshape(DEPTH, 1, N_MOD * D_MODEL)],
        [pl.BlockSpec((COND_ROWS, tn), lambda j: (0, j))],
        [jax.ShapeDtypeStruct((COND_ROWS, n), F32)])


def _mod_table(plan_out, chunks):
    return _Mod(plan_out[0].reshape(1, COND_ROWS, 1, (chunks[1] - chunks[0]) * D_MODEL), chunks[0])


def _norm_mod_body(xc_ref, xl_ref, g_ref, sc_ref, sh_ref, o_ref, *, n_ctx_blocks):
    def emit(x_ref):
        y = _rms(x_ref[...]) * g_ref[...]
        o_ref[...] = (y * (1.0 + sc_ref[...]) + sh_ref[...]).astype(o_ref.dtype)

    owns_ctx = pl.program_id(0) < n_ctx_blocks
    pl.when(owns_ctx)(lambda: emit(xc_ref))
    pl.when(jnp.logical_not(owns_ctx))(lambda: emit(xl_ref))


def _norm_mod(xs, gain, mod, layer, scale_chunk, shift_chunk):
    tm = 1024
    return pl.pallas_call(
        functools.partial(_norm_mod_body, n_ctx_blocks=N_CTX // tm),
        grid=(N_TOK // tm,),
        in_specs=[*_stream_specs((tm, D_MODEL), N_CTX // tm),
                  pl.BlockSpec((None, 1, D_MODEL), lambda i: (layer, 0, 0)),
                  _mod_spec(mod, scale_chunk, tm),
                  _mod_spec(mod, shift_chunk, tm)],
        out_specs=pl.BlockSpec((tm, D_MODEL), lambda i: (i, 0)),
        out_shape=jax.ShapeDtypeStruct((N_TOK, D_MODEL), BF16),
        compiler_params=_cparams(1),
        name="norm_mod",
    )(*xs, gain.reshape(DEPTH, 1, D_MODEL), mod.table, mod.table)


def _in_proj_body(h_ref, wt_ref, o_ref):
    o_ref[...] = _dot_nt(h_ref[...], wt_ref[...].astype(BF16))


def _in_proj(h, w_in_t, layer):
    tm, tn = 2048, 512
    return pl.pallas_call(
        _in_proj_body,
        grid=(N_TOK // tm, pl.cdiv(W_IN, tn)),
        in_specs=[pl.BlockSpec((tm, D_MODEL), lambda i, j: (i, 0)),
                  pl.BlockSpec((None, tn, D_MODEL), lambda i, j: (layer, j, 0))],
        out_specs=pl.BlockSpec((tm, tn), lambda i, j: (i, j)),
        out_shape=jax.ShapeDtypeStruct((N_TOK, W_IN), F32),
        compiler_params=_cparams(2),
        name="in_proj",
    )(h, w_in_t)


class _Plan(NamedTuple):
    name: str
    body: Callable
    grid: tuple
    in_specs: list
    args: list
    out_specs: list
    out_shape: list
    aliases: dict = {}


def _run(*plans):
    n_steps = math.prod(plans[0].grid)
    assert all(math.prod(p.grid) == n_steps for p in plans)

    def remap(spec, grid):
        def flat_map(i, index_map=spec.index_map):
            idx = []
            for size in reversed(grid):
                idx.append(i % size)
                i = i // size
            return index_map(*reversed(idx))
        return pl.BlockSpec(spec.block_shape, flat_map)

    n_in = [len(p.in_specs) for p in plans]
    n_out = [len(p.out_specs) for p in plans]

    def body(*refs):
        ins, outs = refs[:sum(n_in)], refs[sum(n_in):]
        for k, p in enumerate(plans):
            p.body(*ins[sum(n_in[:k]):sum(n_in[:k + 1])], *outs[sum(n_out[:k]):sum(n_out[:k + 1])])

    aliases = {sum(n_in[:k]) + i: sum(n_out[:k]) + o
               for k, p in enumerate(plans) for i, o in p.aliases.items()}
    res = pl.pallas_call(
        body, grid=(n_steps,),
        in_specs=[remap(s, p.grid) for p in plans for s in p.in_specs],
        out_specs=[remap(s, p.grid) for p in plans for s in p.out_specs],
        out_shape=[s for p in plans for s in p.out_shape], input_output_aliases=aliases,
        compiler_params=_cparams(1),
        name="+".join(p.name for p in plans),
    )(*[a for p in plans for a in p.args])
    return [res[sum(n_out[:k]):sum(n_out[:k + 1])] for k in range(len(plans))]


def _head_norm(x, g, split):
    x2 = x * x
    if split:
        lo = lax.broadcasted_iota(jnp.int32, x.shape, 1) < (HEAD // 2)
        s_lo = jnp.sum(jnp.where(lo, x2, 0.0), axis=-1, keepdims=True)
        s_hi = jnp.sum(jnp.where(lo, 0.0, x2), axis=-1, keepdims=True)
        ms = jnp.where(lo, s_lo, s_hi) * (2.0 / HEAD)
    else:
        ms = jnp.mean(x2, axis=-1, keepdims=True)
    return (x * lax.rsqrt(ms + EPS)) * g


def _rope(x, cos, sin_signed):
    even = (lax.broadcasted_iota(jnp.int32, x.shape, 1) % 2) == 0
    swapped = jnp.where(even, pltpu.roll(x, HEAD - 1, axis=1), pltpu.roll(x, 1, axis=1))
    return x * cos + swapped * sin_signed


def _with_ones(v):
    return jnp.concatenate([v.astype(BF16), jnp.ones(v.shape, BF16)], axis=1)


def _softmax_pv(qb, k_pieces, vext_pieces):
    scores = [_dot_nt(qb, kp) for kp in k_pieces]
    m = functools.reduce(jnp.maximum, [jnp.max(s, axis=-1, keepdims=True) for s in scores])
    acc = functools.reduce(lambda a, b: a + b,
                           [_dot(jnp.exp(s - m).astype(BF16), ve) for s, ve in zip(scores, vext_pieces)])
    return acc[:, :HEAD] / acc[:, HEAD:]


def _attn_body(*refs, nkv, nq, diff, latent, stacked, tq, n_sub, lam_init):
    it = iter(refs)
    q_ref, k_ref, v_ref, qg_ref, kg_ref = (next(it) for _ in range(5))
    if latent:
        cos_ref, sin_ref, ck_ref, cv_ref = (next(it) for _ in range(4))
    if diff:
        lam_ref, sub_ref = next(it), next(it)
    if stacked:
        kprev_ref, vprev_ref = next(it), next(it)
    o_ref = next(it)
    if not latent:
        kn_ref, vo_ref = next(it), next(it)

    if diff:
        lv = lam_ref[...]
        lam = (jnp.exp(jnp.sum(lv[0:1] * lv[1:2], axis=-1, keepdims=True))
               - jnp.exp(jnp.sum(lv[2:3] * lv[3:4], axis=-1, keepdims=True)) + lam_init)
        qk_scale = (HEAD // 2) ** -0.5
    else:
        qk_scale = HEAD ** -0.5
    qg = qg_ref[...]

    for kv in range(nkv):
        kcols = slice(kv * HEAD, (kv + 1) * HEAD)
        v = v_ref[:, kcols]
        kn = _head_norm(k_ref[:, kcols], kg_ref[...], diff)
        if not latent:
            if stacked:
                kn_ref[0, kv] = kprev_ref[kv]
                vo_ref[0, kv] = vprev_ref[kv]
                kn_ref[1, kv] = kn
                vo_ref[1, kv] = v
            else:
                kn_ref[kv] = kn
                vo_ref[kv] = v
        if latent:
            kn = _rope(kn, cos_ref[...], sin_ref[...])
        k_pieces = [kn.astype(BF16)]
        vext_pieces = [_with_ones(v)]
        if latent:
            k_pieces.append(ck_ref[kv].astype(BF16))
            vext_pieces.append(_with_ones(cv_ref[kv]))

        def one_tile(rows, j, k_pieces=k_pieces, vext_pieces=vext_pieces, kv=kv):
            cols = slice((kv * nq + j) * HEAD, (kv * nq + j + 1) * HEAD)
            qn = _head_norm(q_ref[rows, cols], qg, diff)
            if latent:
                qn = _rope(qn, cos_ref[rows, :], sin_ref[rows, :])
            qn = qn * qk_scale
            if diff:
                lo = lax.broadcasted_iota(jnp.int32, qn.shape, 1) < (HEAD // 2)
                o1 = _softmax_pv(jnp.where(lo, qn, 0.0).astype(BF16), k_pieces, vext_pieces)
                o2 = _softmax_pv(jnp.where(lo, 0.0, qn).astype(BF16), k_pieces, vext_pieces)
                o = (_rms(o1 - lam * o2) * sub_ref[...]) * (1.0 - lam_init)
            else:
                o = _softmax_pv(qn.astype(BF16), k_pieces, vext_pieces)
            o_ref[rows, cols] = o.astype(o_ref.dtype)

        for j in range(nq):
            for i in range(n_sub):
                one_tile(slice(i * tq, (i + 1) * tq), j)


def _attention(proj, *, layer, latent, diff, q_gain, k_gain, rope_tabs=None,
               cache_k=None, cache_v=None, lam_vecs=None, subln=None, lam_init=0.0, prev_kv=None):
    t = DEC_SEQ if latent else SEQ
    nb = DEC_BATCH if latent else BATCH
    row0 = N_CTX // t if latent else 0
    nq, nkv_all, col0 = (1, DIFF_HEADS, DIFF_COL) if diff else (GQA_GROUP, GQA_KV_HEADS, GQA_COL)
    nkv = 1 if latent else nkv_all
    wq, wk = nkv * nq * HEAD, nkv * HEAD
    qblk = col0 // wq
    kblk = (col0 + nkv_all * nq * HEAD) // wk
    vblk = (col0 + nkv_all * (nq + 1) * HEAD) // wk
    tq = 256
    n_sub = t // tq
    grid = (nb, nkv_all) if latent else (nb,)
    hix = (lambda g: g[1]) if latent else (lambda g: 0)

    def const(*shape):
        return pl.BlockSpec(shape, lambda *g: (0,) * len(shape))

    in_specs = [pl.BlockSpec((t, wq), lambda *g: (row0 + g[0], qblk + hix(g))),
                pl.BlockSpec((t, wk), lambda *g: (row0 + g[0], kblk + hix(g))),
                pl.BlockSpec((t, wk), lambda *g: (row0 + g[0], vblk + hix(g))),
                const(1, HEAD), const(1, HEAD)]
    args = [proj, proj, proj, q_gain, k_gain]
    if latent:
        cspec = pl.BlockSpec((None, None, 1, PAST_LEN, HEAD), lambda *g: (g[0], layer, g[1], 0, 0))
        in_specs += [const(t, HEAD), const(t, HEAD), cspec, cspec]
        args += [*rope_tabs, cache_k, cache_v]
    if diff:
        in_specs += [const(4, HEAD // 2), const(1, HEAD)]
        args += [lam_vecs, subln]
    stacked = prev_kv is not None
    if stacked:
        pspec = pl.BlockSpec((None, nkv, t, HEAD), lambda *g: (g[0], 0, 0, 0))
        in_specs += [pspec, pspec]
        args += list(prev_kv)

    out_specs = [pl.BlockSpec((t, wq), lambda *g: (g[0], hix(g)))]
    out_shape = [jax.ShapeDtypeStruct((nb * t, BRANCH_WIDTH), BF16)]
    if not latent:
        if stacked:
            kv_spec = pl.BlockSpec((DEPTH, None, nkv, t, HEAD), lambda *g: (0, g[0], 0, 0, 0))
            kv_shape = jax.ShapeDtypeStruct((DEPTH, nb, nkv, t, HEAD), F32)
        else:
            kv_spec = pl.BlockSpec((None, nkv, t, HEAD), lambda *g: (g[0], 0, 0, 0))
            kv_shape = jax.ShapeDtypeStruct((nb, nkv, t, HEAD), F32)
        out_specs += [kv_spec, kv_spec]
        out_shape += [kv_shape, kv_shape]

    body = functools.partial(_attn_body, nkv=nkv, nq=nq, diff=diff, latent=latent, stacked=stacked,
                             tq=tq, n_sub=n_sub, lam_init=lam_init)
    name = ("diff" if diff else "gqa") + ("_lat" if latent else "_ctx")
    return _Plan(name, body, grid, in_specs, args, out_specs, out_shape)


def _log_sigmoid(x):
    return jnp.minimum(x, 0.0) - jnp.log1p(jnp.exp(-jnp.abs(x)))


def _seg_bcast(x, width, pos, needed):
    n = x.shape[0]
    if width >= 8:
        g = x.reshape(n // width, width, x.shape[1])[:, pos:pos + 1, :]
        return jnp.broadcast_to(g, (n // width, width, x.shape[1])).reshape(x.shape)
    off = lax.broadcasted_iota(jnp.int32, (n, 1), 0) % width
    y = None
    for j in needed:
        cand = x if j == pos else pltpu.roll(x, (j - pos) % n, axis=0)
        y = cand if y is None else jnp.where(off == j, cand, y)
    return y


def _gla_block(q, k, a_f, a_b, mask_ref):
    n = q.shape[0]
    row = lax.broadcasted_iota(jnp.int32, (n, 1), 0)
    pf, qf, pb, qb = a_f, a_f, a_b, a_b
    acc = _dot_nt(q.astype(BF16), k.astype(BF16)) * mask_ref[0]
    half, level = 1, 1
    while half < n:
        width = 2 * half
        right = (row % width) >= half
        eq = jnp.where(right, pf, qb)
        ek = jnp.where(right, pb - a_b, qf - a_f)
        s = _dot_nt((q * jnp.exp(eq)).astype(BF16), (k * jnp.exp(ek)).astype(BF16))
        acc = acc + s * mask_ref[level]
        lo_pos, hi_pos = list(range(half)), list(range(half, width))
        pf = pf + jnp.where(right, _seg_bcast(pf, width, half - 1, hi_pos), 0.0)
        pb = pb + jnp.where(right, _seg_bcast(pb, width, half - 1, hi_pos), 0.0)
        qf = qf + jnp.where(right, 0.0, _seg_bcast(qf, width, half, lo_pos))
        qb = qb + jnp.where(right, 0.0, _seg_bcast(qb, width, half, lo_pos))
        half, level = width, level + 1
    return acc, pf, qf, pb, qb


def _gla_body(*refs, n_blocks, n_heads, latent, stacked):
    it = iter(refs)
    q_ref, k_ref, v_ref, r_ref, z_ref, wa_ref, ba_ref, gn_ref, mask_ref = (next(it) for _ in range(9))
    if latent:
        s0f_ref, s0b_ref = next(it), next(it)
    if stacked:
        sfprev_ref, sbprev_ref = next(it), next(it)
    o_ref = next(it)
    if not latent:
        sf_ref, sb_ref = next(it), next(it)
    n = GLA_BLOCK
    wa = wa_ref[...].astype(BF16)
    ba = ba_ref[...]

    for hh in range(n_heads):
        kcols = slice(hh * GLA_DK, (hh + 1) * GLA_DK)
        vcols = slice(hh * GLA_DV, (hh + 1) * GLA_DV)
        blocks = []
        for c in range(n_blocks):
            rows = slice(c * n, (c + 1) * n)
            real = lax.broadcasted_iota(jnp.int32, (n, HEAD), 1) < (W_IN - LA_COL)
            z = jnp.where(real, z_ref[rows, :], 0.0).astype(BF16)
            a_f = _log_sigmoid(_dot(z, wa[0][:, kcols]) + ba[0:1, kcols]) * (1.0 / GLA_TAU)
            a_b = _log_sigmoid(_dot(z, wa[1][:, kcols]) + ba[1:2, kcols]) * (1.0 / GLA_TAU)
            q = q_ref[rows, kcols] * (GLA_DK ** -0.5)
            k = k_ref[rows, kcols]
            vb = v_ref[rows, vcols].astype(BF16)
            acc, pf, qf, pb, qb = _gla_block(q, k, a_f, a_b, mask_ref)
            o = _dot(acc.astype(BF16), vb)
            kf = (k * jnp.exp(qf - a_f)).T.astype(BF16)
            kb = (k * jnp.exp(pb - a_b)).T.astype(BF16)
            blocks.append(dict(o=o, u_f=_dot(kf, vb), u_b=_dot(kb, vb),
                               q_f=(q * jnp.exp(pf)).astype(BF16), q_b=(q * jnp.exp(qb)).astype(BF16),
                               tot_f=pf.T[:, n - 1:n], tot_b=qb.T[:, 0:1]))

        if latent:
            s = s0f_ref[hh]
            for c in range(n_blocks):
                blk = blocks[c]
                blk["o"] = blk["o"] + _dot(blk["q_f"], s.astype(BF16))
                s = jnp.exp(blk["tot_f"]) * s + blk["u_f"]
            s = s0b_ref[hh]
            for c in reversed(range(n_blocks)):
                blk = blocks[c]
                blk["o"] = blk["o"] + _dot(blk["q_b"], s.astype(BF16))
                s = jnp.exp(blk["tot_b"]) * s + blk["u_b"]
        elif stacked:
            sf_ref[0, hh] = sfprev_ref[hh]
            sb_ref[0, hh] = sbprev_ref[hh]
            sf_ref[1, hh] = blocks[0]["u_f"]
            sb_ref[1, hh] = blocks[0]["u_b"]
        else:
            sf_ref[hh] = blocks[0]["u_f"]
            sb_ref[hh] = blocks[0]["u_b"]

        for c in range(n_blocks):
            rows = slice(c * n, (c + 1) * n)
            r = r_ref[rows, vcols]
            out = (_rms(blocks[c]["o"]) * gn_ref[...]) * (r * jax.nn.sigmoid(r))
            o_ref[rows, vcols] = out.astype(o_ref.dtype)


def _gla_masks():
    n = GLA_BLOCK
    t = jnp.arange(n)[:, None]
    s = jnp.arange(n)[None, :]
    masks = [2.0 * (t == s)]
    half = 1
    while half < n:
        width = 2 * half
        masks.append((t // width == s // width) & ((t % width >= half) != (s % width >= half)))
        half = width
    return jnp.stack([m.astype(F32) for m in masks])


def _gla(proj, wa_pad, ba, gnorm, masks, *, layer, latent, s0_f=None, s0_b=None,
         prev_states=None):
    t = DEC_SEQ if latent else SEQ
    nb = DEC_BATCH if latent else BATCH
    row0 = N_CTX // t if latent else 0
    nh = 2
    wk, wv = nh * GLA_DK, nh * GLA_DV
    qc, kc = 36 * GLA_DK // wk, 40 * GLA_DK // wk
    vc, rc = 22 * GLA_DV // wv, 26 * GLA_DV // wv
    n_lvl = masks.shape[0]
    in_specs = [pl.BlockSpec((t, wk), lambda b, h: (row0 + b, qc + h)),
                pl.BlockSpec((t, wk), lambda b, h: (row0 + b, kc + h)),
                pl.BlockSpec((t, wv), lambda b, h: (row0 + b, vc + h)),
                pl.BlockSpec((t, wv), lambda b, h: (row0 + b, rc + h)),
                pl.BlockSpec((t, HEAD), lambda b, h: (row0 + b, LA_COL // HEAD)),
                pl.BlockSpec((None, 2, HEAD, wk), lambda b, h: (layer, 0, 0, h)),
                pl.BlockSpec((None, 2, wk), lambda b, h: (layer, 0, h)),
                pl.BlockSpec((None, 1, GLA_DV), lambda b, h: (layer, 0, 0)),
                pl.BlockSpec((n_lvl, GLA_BLOCK, GLA_BLOCK), lambda b, h: (0, 0, 0))]
    args = [proj, proj, proj, proj, proj, wa_pad, ba, gnorm.reshape(DEPTH, 1, GLA_DV), masks]
    if latent:
        sspec = pl.BlockSpec((None, None, nh, GLA_DK, GLA_DV), lambda b, h: (b, layer, h, 0, 0))
        in_specs += [sspec, sspec]
        args += [s0_f, s0_b]
    stacked = prev_states is not None
    if stacked:
        pspec = pl.BlockSpec((None, nh, GLA_DK, GLA_DV), lambda b, h: (b, h, 0, 0))
        in_specs += [pspec, pspec]
        args += list(prev_states)
    out_specs = [pl.BlockSpec((t, wv), lambda b, h: (b, h))]
    out_shape = [jax.ShapeDtypeStruct((nb * t, BRANCH_WIDTH), BF16)]
    if not latent:
        if stacked:
            st_spec = pl.BlockSpec((DEPTH, None, nh, GLA_DK, GLA_DV), lambda b, h: (0, b, h, 0, 0))
            st_shape = jax.ShapeDtypeStruct((DEPTH, nb, GLA_HEADS, GLA_DK, GLA_DV), F32)
        else:
            st_spec = pl.BlockSpec((None, nh, GLA_DK, GLA_DV), lambda b, h: (b, h, 0, 0))
            st_shape = jax.ShapeDtypeStruct((nb, GLA_HEADS, GLA_DK, GLA_DV), F32)
        out_specs += [st_spec, st_spec]
        out_shape += [st_shape, st_shape]
    body = functools.partial(_gla_body, n_blocks=t // GLA_BLOCK, n_heads=nh, latent=latent,
                             stacked=stacked)
    return _Plan("gla_lat" if latent else "gla_ctx", body, (nb, GLA_HEADS // nh), in_specs, args,
                 out_specs, out_shape)


def _merge_body(h_ref, od_ref, og_ref, ol_ref, wg0_ref, wg1_ref, wg2_ref, b0_ref, b1_ref, b2_ref,
                wb_ref, o_ref):
    h = h_ref[...]
    acc = None
    for n, (br_ref, wg_ref, b_ref) in enumerate(((od_ref, wg0_ref, b0_ref), (og_ref, wg1_ref, b1_ref),
                                                 (ol_ref, wg2_ref, b2_ref))):
        gate = jax.nn.sigmoid(_dot(h, wg_ref[...].astype(BF16)) + b_ref[...])
        term = gate * _dot(br_ref[...], wb_ref[n].astype(BF16))
        acc = term if acc is None else acc + term
    o_ref[...] = acc.astype(o_ref.dtype)


def _merge(h, branches, w_gate, b_gate, w_branch, *, layer, latent):
    tm, tn = 1024, 256
    nj = D_MODEL // tn
    n_rows = N_LAT if latent else N_CTX
    row0 = N_CTX // tm if latent else 0
    b3 = b_gate.reshape(DEPTH, 1, 3 * D_MODEL)
    br_spec = pl.BlockSpec((tm, BRANCH_WIDTH), lambda i, j: (i, 0))
    wg_specs = [pl.BlockSpec((None, D_MODEL, tn), lambda i, j, n=n: (layer, 0, n * nj + j)) for n in range(3)]
    b_specs = [pl.BlockSpec((None, 1, tn), lambda i, j, n=n: (layer, 0, n * nj + j)) for n in range(3)]
    return pl.pallas_call(
        _merge_body,
        grid=(n_rows // tm, nj),
        in_specs=[pl.BlockSpec((tm, D_MODEL), lambda i, j: (row0 + i, 0)), br_spec, br_spec, br_spec,
                  *wg_specs, *b_specs,
                  pl.BlockSpec((None, 3, BRANCH_WIDTH, tn), lambda i, j: (layer, 0, 0, j))],
        out_specs=pl.BlockSpec((tm, tn), lambda i, j: (i, j)),
        out_shape=jax.ShapeDtypeStruct((n_rows, D_MODEL), BF16),
        compiler_params=_cparams(2),
        name="merge_lat" if latent else "merge_ctx",
    )(h, *branches, w_gate, w_gate, w_gate, b3, b3, b3, w_branch)


def _out_proj_body(m_ref, w_ref, x_ref, *rest):
    *g_refs, o_ref = rest
    y = _dot(m_ref[...], w_ref[...].astype(BF16))
    rows = y.shape[0] // len(g_refs)
    for k, g_ref in enumerate(g_refs):
        sl = slice(k * rows, (k + 1) * rows)
        o_ref[sl, :] = x_ref[sl, :] + g_ref[...] * y[sl, :]


def _out_proj(m, w_out, x, mod, *, layer, latent):
    tm, tn = 2048, 512
    nj = D_MODEL // tn
    n_rows = N_LAT if latent else N_CTX
    row0 = N_CTX if latent else 0
    n_gates = tm // DEC_SEQ
    g_specs = [pl.BlockSpec((None, None, 1, tn),
                            lambda i, j, k=k: (0, _cond_row(row0 + i * tm + k * DEC_SEQ), 0,
                                               (2 - mod.chunk0) * nj + j))
               for k in range(n_gates)]
    return pl.pallas_call(
        _out_proj_body,
        grid=(n_rows // tm, nj),
        in_specs=[pl.BlockSpec((tm, D_MODEL), lambda i, j: (i, 0)),
                  pl.BlockSpec((None, D_MODEL, tn), lambda i, j: (layer, 0, j)),
                  pl.BlockSpec((tm, tn), lambda i, j: (i, j)),
                  *g_specs],
        out_specs=pl.BlockSpec((tm, tn), lambda i, j: (i, j)),
        out_shape=jax.ShapeDtypeStruct((n_rows, D_MODEL), F32),
        compiler_params=_cparams(2),
        name="out_proj_lat" if latent else "out_proj_ctx",
    )(m, w_out, x, *([mod.table] * n_gates))


def _route_body(x_ref, g_ref, sc_ref, sh_ref, wr_ref, xg_ref, st_ref, ws_ref,
                hb_ref, afft_ref, aff_ref, *, t, cap, group):
    g = pl.program_id(1)

    @pl.when(g == 0)
    def _():
        y = _rms(x_ref[...]) * g_ref[...]
        hb = (y * (1.0 + sc_ref[...]) + sh_ref[...]).astype(BF16)
        hb_ref[...] = hb
        logits = _dot_nt(wr_ref[...].astype(BF16), hb)
        e = jnp.exp(logits - jnp.max(logits, axis=0, keepdims=True))
        aff_t = e / jnp.sum(e, axis=0, keepdims=True)
        afft_ref[...] = aff_t
        aff_ref[...] = aff_t.T

    blk = HEAD
    nblk = t // blk
    earlier = (lax.broadcasted_iota(jnp.int32, (blk, blk), 0) < lax.broadcasted_iota(jnp.int32, (blk, blk), 1))
    slot = lax.broadcasted_iota(jnp.int32, (cap, t), 0).astype(F32)

    def count(mask):
        return jnp.sum(mask.astype(F32), axis=0, keepdims=True)

    def rank_of(ex):
        col = jnp.broadcast_to(aff_ref[:, ex:ex + 1], (t, blk))
        parts = []
        for j in range(nblk):
            row = afft_ref[ex:ex + 1, j * blk:(j + 1) * blk]
            cd = col[j * blk:(j + 1) * blk]
            cnt = count((cd > row) | ((cd == row) & earlier))
            if j > 0:
                cnt = cnt + count(col[:j * blk] >= row)
            if j < nblk - 1:
                cnt = cnt + count(col[(j + 1) * blk:] > row)
            parts.append(cnt)
        return jnp.concatenate(parts, axis=1)

    for k in range(N_EXPERTS // group):
        @pl.when(g == k)
        def _(k=k):
            sels = []
            for i in range(group):
                ex = k * group + i
                sel = (rank_of(ex) == slot).astype(F32)
                ws_ref[i] = jnp.sum(sel * afft_ref[ex:ex + 1, :], axis=1, keepdims=True)
                sels.append(sel)
            sel_g = jnp.concatenate(sels, axis=0)
            xg = _dot(sel_g.astype(BF16), hb_ref[...]).astype(BF16)
            xg_ref[...] = xg.reshape(group, cap, D_MODEL)
            st_ref[...] = sel_g.T.astype(BF16)


def _route(x, gain, mod, w_router_t, *, layer, latent):
    t = DEC_SEQ if latent else SEQ
    nb = DEC_BATCH if latent else BATCH
    row0 = N_CTX // t if latent else 0
    cap = 2 * t // N_EXPERTS
    group = 512 // cap
    return pl.pallas_call(
        functools.partial(_route_body, t=t, cap=cap, group=group),
        grid=(nb, N_EXPERTS // group),
        in_specs=[pl.BlockSpec((t, D_MODEL), lambda b, g: (b, 0)),
                  pl.BlockSpec((None, 1, D_MODEL), lambda b, g: (layer, 0, 0)),
                  pl.BlockSpec((None, None, 1, D_MODEL),
                               lambda b, g: (0, _cond_row((row0 + b) * t), 0, 4 - mod.chunk0)),
                  pl.BlockSpec((None, None, 1, D_MODEL),
                               lambda b, g: (0, _cond_row((row0 + b) * t), 0, 3 - mod.chunk0)),
                  pl.BlockSpec((None, N_EXPERTS, D_MODEL), lambda b, g: (layer, 0, 0))],
        out_specs=[pl.BlockSpec((group, cap, D_MODEL), lambda b, g: (g, b, 0)),
                   pl.BlockSpec((t, group * cap), lambda b, g: (b, g)),
                   pl.BlockSpec((group, cap, 1), lambda b, g: (g, b, 0))],
        out_shape=[jax.ShapeDtypeStruct((N_EXPERTS, nb * cap, D_MODEL), BF16),
                   jax.ShapeDtypeStruct((nb * t, N_EXPERTS * cap), BF16),
                   jax.ShapeDtypeStruct((N_EXPERTS, nb * cap, 1), F32)],
        scratch_shapes=[pltpu.VMEM((t, D_MODEL), BF16), pltpu.VMEM((N_EXPERTS, t), F32),
                        pltpu.VMEM((t, N_EXPERTS), F32)],
        compiler_params=_cparams(2),
        name="route_lat" if latent else "route_ctx",
    )(x, gain.reshape(DEPTH, 1, D_MODEL), mod.table, mod.table, w_router_t)


def _experts_body(xc_ref, xl_ref, wsc_ref, wsl_ref, wg_ref, wu_ref, wd_ref, o_ref, acc_ref, *, nf, half):
    f = pl.program_id(1)

    def step(first, last):
        wg = wg_ref[...].astype(BF16)
        wu = wu_ref[...].astype(BF16)
        wd = wd_ref[...].astype(BF16)
        for s, (x_ref, ws_ref) in enumerate(((xc_ref, wsc_ref), (xl_ref, wsl_ref))):
            rows = slice(s * half, (s + 1) * half)
            x = x_ref[...]
            g = _dot(x, wg)
            hid = (g * jax.nn.sigmoid(g)) * _dot(x, wu)
            y = _dot(hid.astype(BF16), wd)
            if not first:
                y = acc_ref[rows, :] + y
            if last:
                o_ref[rows, :] = (y * ws_ref[...]).astype(o_ref.dtype)
            else:
                acc_ref[rows, :] = y

    pl.when(f == 0)(lambda: step(True, False))
    pl.when((f > 0) & (f < nf - 1))(lambda: step(False, False))
    pl.when(f == nf - 1)(lambda: step(False, True))


def _experts(xg_ctx, xg_lat, ws_ctx, ws_lat, w_gate, w_up, w_down, layer):
    half = xg_ctx.shape[1]
    tf = 512
    nf = D_MODEL // tf
    x_spec = pl.BlockSpec((None, half, D_MODEL), lambda e, f: (e, 0, 0))
    ws_spec = pl.BlockSpec((None, half, 1), lambda e, f: (e, 0, 0))
    return pl.pallas_call(
        functools.partial(_experts_body, nf=nf, half=half),
        grid=(N_EXPERTS, nf),
        in_specs=[x_spec, x_spec, ws_spec, ws_spec,
                  pl.BlockSpec((None, None, D_MODEL, tf), lambda e, f: (layer, e, 0, f)),
                  pl.BlockSpec((None, None, D_MODEL, tf), lambda e, f: (layer, e, 0, f)),
                  pl.BlockSpec((None, None, tf, D_MODEL), lambda e, f: (layer, e, f, 0))],
        out_specs=pl.BlockSpec((None, 2 * half, D_MODEL), lambda e, f: (e, 0, 0)),
        out_shape=jax.ShapeDtypeStruct((N_EXPERTS, 2 * half, D_MODEL), BF16),
        scratch_shapes=[pltpu.VMEM((2 * half, D_MODEL), F32)],
        compiler_params=_cparams(2),
        name="experts",
    )(xg_ctx, xg_lat, ws_ctx, ws_lat, w_gate, w_up, w_down)


def _combine_body(st_ref, ye_ref, x_ref, g_ref, o_ref, *, cap):
    ye = ye_ref[...].reshape(N_EXPERTS * cap, ye_ref.shape[-1])
    o_ref[...] = x_ref[...] + g_ref[...] * _dot(st_ref[...], ye)


def _combine(st, ye, x, mod, *, layer, latent):
    t = DEC_SEQ if latent else SEQ
    nb = DEC_BATCH if latent else BATCH
    cap = 2 * t // N_EXPERTS
    row0 = N_CTX // t if latent else 0
    slot0 = (BATCH * 2 * SEQ // N_EXPERTS) // cap if latent else 0
    tn = 512 if latent else D_MODEL
    nj = D_MODEL // tn
    return _Plan(
        "combine_lat" if latent else "combine_ctx", functools.partial(_combine_body, cap=cap), (nb, nj),
        [pl.BlockSpec((t, N_EXPERTS * cap), lambda b, j: (b, 0)),
         pl.BlockSpec((N_EXPERTS, cap, tn), lambda b, j: (0, slot0 + b, j)),
         pl.BlockSpec((t, tn), lambda b, j: (b, j)),
         pl.BlockSpec((None, None, 1, tn),
                      lambda b, j: (0, _cond_row((row0 + b) * t), 0, (5 - mod.chunk0) * nj + j))],
        [st, ye, x, mod.table],
        [pl.BlockSpec((t, tn), lambda b, j: (b, j))],
        [jax.ShapeDtypeStruct((nb * t, D_MODEL), F32)],
        {2: 0})


def _rope_tables(t, dim):
    rows = t // GRID_W
    row = jnp.repeat(jnp.arange(rows), GRID_W)
    col = jnp.tile(jnp.arange(GRID_W), rows)

    def angles(pos, d):
        inv = ROPE_THETA ** (-jnp.arange(0, d, 2, dtype=F32) / d)
        return pos.astype(F32)[:, None] * inv[None, :]

    ang = jnp.concatenate([angles(row, dim // 2), angles(col, dim // 2)], axis=-1)
    cos = jnp.repeat(jnp.cos(ang), 2, axis=-1)
    sin = jnp.repeat(jnp.sin(ang), 2, axis=-1) * jnp.tile(jnp.array([-1.0, 1.0], F32), dim // 2)
    reps = HEAD // dim
    return jnp.tile(cos, (1, reps)), jnp.tile(sin, (1, reps))


def kernel(x_prompt, x_sample, cache_diff_k, cache_diff_v, cache_gqa_k, cache_gqa_v, state_gla_fwd,
           state_gla_bwd, c, c_ctx, w_mod, b_mod, norm_mix, w_in, diff_q_norm, diff_k_norm,
           diff_lambda, diff_subln, gqa_q_norm, gqa_k_norm, gla_wa2, gla_ba, gla_norm, w_branch,
           w_gate, b_gate, w_out, norm_ffn, w_router, w_e_gate, w_e_up, w_e_down):
    xs = (x_prompt.reshape(N_CTX, D_MODEL), x_sample.reshape(N_LAT, D_MODEL))
    cond = jnp.concatenate([c_ctx[None, :], c, jnp.zeros((COND_ROWS - 1 - DEC_BATCH, D_MODEL), F32)], axis=0)
    pre, post, full = (0, 2), (2, N_MOD), (0, N_MOD)
    mod_in = _mod_table(_run(_modulation(cond, w_mod, b_mod, layer=0, chunks=pre, n_steps=4))[0], pre)
    mod_rest = None

    rope_d = _rope_tables(DEC_SEQ, HEAD // 2)
    rope_g = _rope_tables(DEC_SEQ, HEAD)
    masks = _gla_masks()
    def zeros(rows):
        return jnp.zeros((DEPTH, rows, GLA_HEADS * GLA_DK), F32)
    wa_pad = jnp.stack(
        [jnp.concatenate([gla_wa2[:, 0], zeros(HEAD - GLA_GATE_RANK)], axis=1),
         jnp.concatenate([zeros(GLA_GATE_RANK), gla_wa2[:, 1], zeros(HEAD - 2 * GLA_GATE_RANK)], axis=1)],
        axis=1)
    dq_gain = jnp.tile(diff_q_norm, (1, 2))
    dk_gain = jnp.tile(diff_k_norm, (1, 2))
    w_router_t = jnp.swapaxes(w_router, 1, 2)
    w_in_t = jnp.swapaxes(w_in, 1, 2)

    diff_kv = gqa_kv = gla_st = None
    for l in range(DEPTH):
        lam_init = 0.8 - 0.6 * math.exp(-0.3 * l)
        h = _norm_mod(xs, norm_mix, mod_in, l, 1, 0)
        proj = _in_proj(h, w_in_t, l)

        dkw = dict(layer=l, diff=True, q_gain=dq_gain[l:l + 1], k_gain=dk_gain[l:l + 1],
                   lam_vecs=diff_lambda[l], subln=diff_subln[l:l + 1], lam_init=lam_init)
        gkw = dict(layer=l, diff=False, q_gain=gqa_q_norm[l:l + 1], k_gain=gqa_k_norm[l:l + 1])
        plans = [_attention(proj, latent=False, prev_kv=diff_kv, **dkw),
                 _attention(proj, latent=False, prev_kv=gqa_kv, **gkw)]
        if mod_rest is None:
            plans.append(_modulation(cond, w_mod, b_mod, layer=l, chunks=post, n_steps=math.prod(plans[0].grid)))
        (od_c, *diff_kv), (og_c, *gqa_kv), *rest_mod = _run(*plans)
        if rest_mod:
            mod_rest = _mod_table(rest_mod[0], post)
        plans = [
            _gla(proj, wa_pad, gla_ba, gla_norm, masks, layer=l, latent=False, prev_states=gla_st),
            _attention(proj, latent=True, rope_tabs=rope_d, cache_k=cache_diff_k, cache_v=cache_diff_v, **dkw)]
        if l + 1 < DEPTH:
            plans.append(_modulation(cond, w_mod, b_mod, layer=l + 1, chunks=full,
                                     n_steps=math.prod(plans[0].grid)))
        (ol_c, *gla_st), (od_l,), *next_mod = _run(*plans)
        (ol_l,), (og_l,) = _run(
            _gla(proj, wa_pad, gla_ba, gla_norm, masks, layer=l, latent=True,
                 s0_f=state_gla_fwd, s0_b=state_gla_bwd),
            _attention(proj, latent=True, rope_tabs=rope_g, cache_k=cache_gqa_k, cache_v=cache_gqa_v, **gkw))

        branches = ((od_c, og_c, ol_c), (od_l, og_l, ol_l))
        routed = []
        for latent in (False, True):
            m = _merge(h, branches[latent], w_gate, b_gate, w_branch, layer=l, latent=latent)
            x = _out_proj(m, w_out, xs[latent], mod_rest, layer=l, latent=latent)
            routed.append((x, *_route(x, norm_ffn, mod_rest, w_router_t, layer=l, latent=latent)))
        (x_c, xg_c, st_c, ws_c), (x_l, xg_l, st_l, ws_l) = routed
        ye = _experts(xg_c, xg_l, ws_c, ws_l, w_e_gate, w_e_up, w_e_down, l)
        (x_c,), (x_l,) = _run(_combine(st_c, ye, x_c, mod_rest, layer=l, latent=False),
                              _combine(st_l, ye, x_l, mod_rest, layer=l, latent=True))
        xs = (x_c, x_l)
        if next_mod:
            mod_in = mod_rest = _mod_table(next_mod[0], full)

    caches = tuple(jnp.swapaxes(a, 0, 1) for a in (*diff_kv, *gqa_kv, *gla_st))
    return (xs[0].reshape(BATCH, SEQ, D_MODEL), xs[1].reshape(DEC_BATCH, DEC_SEQ, D_MODEL)) + caches
```
